```python
import jax, jax.numpy as jnp
from jax import lax
import numpy as np

D_MODEL = 1024
BATCH = 8
SEQ = 2048
DEPTH = 2
DEC_BATCH = 128
DEC_SEQ = 4
PAST_LEN = 16384
PAGE_SIZE = 128

HEAD_A = 64
D_A = D_MODEL
N_HEADS_A = D_A // HEAD_A
D_DECAY_LORA = max(32, round(1.8 * D_MODEL ** 0.5 / 32) * 32)
D_AAA_LORA = max(32, round(1.8 * D_MODEL ** 0.5 / 32) * 32)
D_GATE_LORA = max(32, round(0.6 * D_MODEL ** 0.8 / 32) * 32)
GN_EPS = 64e-5
D_B = D_MODEL
CHUNK = 128
N_GROUPS_B = 8
GROUP_B = D_B // N_GROUPS_B
D_FF = 4 * D_MODEL
D_PLE = 256
N_IN = 3 * D_A + 2 * D_B + 2 * D_MODEL
EPS = 1e-6

kernel_name = "rwkv7_chunk_gmlp_hybrid_step"


def _rmsnorm(x, g):
    xf = x.astype(jnp.float32)
    y = xf * lax.rsqrt(jnp.mean(xf * xf, -1, keepdims=True) + EPS)
    return (y * g.astype(jnp.float32)).astype(x.dtype)


def _layernorm(x, g, b):
    xf = x.astype(jnp.float32)
    mu = jnp.mean(xf, -1, keepdims=True)
    var = jnp.mean(jnp.square(xf - mu), -1, keepdims=True)
    y = (xf - mu) * lax.rsqrt(var + EPS)
    return (y * g.astype(jnp.float32) + b.astype(jnp.float32)).astype(x.dtype)


def _head_norm(y, g, b):
    mu = jnp.mean(y, -1, keepdims=True)
    var = jnp.mean(jnp.square(y - mu), -1, keepdims=True)
    yn = (y - mu) * lax.rsqrt(var + GN_EPS)
    return yn * g.astype(jnp.float32).reshape(N_HEADS_A, HEAD_A) + b.astype(jnp.float32).reshape(N_HEADS_A, HEAD_A)


def _wkv_scan(r, w, k, v, kk, a, S0):
    def step(S, inp):
        r_t, w_t, k_t, v_t, kk_t, a_t = inp
        sa = jnp.einsum('bhij,bhj->bhi', S, -kk_t)
        S = (S * w_t[:, :, None, :] + sa[..., None] * (kk_t * a_t)[:, :, None, :]
             + v_t[..., None] * k_t[:, :, None, :])
        return S, jnp.einsum('bhij,bhj->bhi', S, r_t)
    xs = tuple(jnp.moveaxis(t, 1, 0) for t in (r, w, k, v, kk, a))
    S, y = lax.scan(step, S0.astype(jnp.float32), xs)
    return S, jnp.moveaxis(y, 0, 1)


def _chunk_mix(vn, w_s, b_s):
    B, L, _ = vn.shape
    n_chunks = -(-L // CHUNK)
    pad = n_chunks * CHUNK - L
    vp = jnp.pad(vn, ((0, 0), (0, pad), (0, 0))).reshape(B, n_chunks, CHUNK, N_GROUPS_B, GROUP_B)
    ws = w_s * jnp.tril(jnp.ones((CHUNK, CHUNK), w_s.dtype))
    s = jnp.einsum('gts,bnsgc->bntgc', ws, vp) + b_s.T[:, :, None]
    return s.reshape(B, n_chunks * CHUNK, D_B)[:, :L]


def _layer(h, p_l, wkv0, shift0, mix_pre_g, mix_post_g, ffn_pre_g, ffn_post_g,
           w_in, mu_rkv, mu_wag, w0, w1, w2, a0, a1, a2, g1, g2, k_k, k_a, r_k,
           lnx_g, lnx_b, vn_g, vn_b, w_s, b_s, w_out, w_up, w_down, w_pe, w_pg):
    f32 = jnp.float32
    B, L, _ = h.shape
    xn = _rmsnorm(h, mix_pre_g)
    sh0 = shift0.astype(xn.dtype)
    xs = jnp.concatenate([sh0[:, None, :], xn[:, :-1]], axis=1)
    proj = xn @ w_in
    rkv = proj[..., :3 * D_A]
    rkv_first = sh0 @ w_in[:, :3 * D_A]
    rkv_prev = jnp.concatenate([rkv_first[:, None], rkv[:, :-1]], axis=1)
    rkv = rkv + (rkv_prev - rkv) * mu_rkv
    r, k, v = jnp.split(rkv, 3, axis=-1)
    gu, gv, ga, gb = jnp.split(proj[..., 3 * D_A:], [D_B, 2 * D_B, 2 * D_B + D_MODEL], axis=-1)
    dx = xs - xn
    xw = xn + dx * mu_wag[0]
    xa = xn + dx * mu_wag[1]
    xg = xn + dx * mu_wag[2]
    w_log = -jax.nn.softplus(-(w0 + jnp.tanh(xw @ w1) @ w2).astype(f32)) - 0.5
    decay = jnp.exp(-jnp.exp(w_log))
    a = jax.nn.sigmoid((a0 + (xa @ a1) @ a2).astype(f32))
    g = jax.nn.sigmoid(xg @ g1) @ g2
    hd = lambda t: t.astype(f32).reshape(B, L, N_HEADS_A, HEAD_A)
    ph = lambda t: t.astype(f32).reshape(N_HEADS_A, HEAD_A)
    rf, kf, vf, a_h = hd(r), hd(k), hd(v), hd(a)
    kk = kf * ph(k_k)
    kk = kk / jnp.maximum(jnp.sqrt(jnp.sum(kk * kk, -1, keepdims=True)), 1e-12)
    kf = kf * (1.0 + (a_h - 1.0) * ph(k_a))
    S, y = _wkv_scan(rf, hd(decay), kf, vf, kk, a_h, wkv0)
    y = _head_norm(y, lnx_g, lnx_b)
    y = y + jnp.sum(rf * kf * r_k.astype(f32), -1, keepdims=True) * vf
    o_a = y.reshape(B, L, D_A).astype(h.dtype) * g
    u = jax.nn.gelu(gu, approximate=False)
    vn = _layernorm(jax.nn.gelu(gv, approximate=False), vn_g, vn_b)
    o_b = u * _chunk_mix(vn, w_s, b_s)
    mixed = jax.nn.sigmoid(ga) * o_a + jax.nn.sigmoid(gb) * o_b
    h = h + _rmsnorm(mixed @ w_out, mix_post_g)
    z = _rmsnorm(h, ffn_pre_g) @ w_up
    h = h + _rmsnorm(jnp.square(jax.nn.relu(z)) @ w_down, ffn_post_g)
    h = h + jax.nn.sigmoid(h @ w_pg) * (p_l @ w_pe)
    return h, S, xn[:, -1], vn


def setup_inputs(seed: int = 0) -> dict:
    key = jax.random.key(seed)
    ks = iter(jax.random.split(key, 40))
    nrm = lambda shape, s: s * jax.random.normal(next(ks), shape, jnp.float32)
    uni = lambda shape, lo, hi: jax.random.uniform(next(ks), shape, jnp.float32, lo, hi)
    L, D = DEPTH, D_MODEL
    return {
        "x_prompt": nrm((BATCH, SEQ, D), 1.0),
        "x_sample": nrm((DEC_BATCH, DEC_SEQ, D), 1.0),
        "state_wkv": nrm((L, DEC_BATCH, N_HEADS_A, HEAD_A, HEAD_A), 0.5),
        "state_shift": nrm((L, DEC_BATCH, D), 1.0),
        "p_prompt": nrm((L, BATCH, SEQ, D_PLE), 1.0),
        "p_sample": nrm((L, DEC_BATCH, DEC_SEQ, D_PLE), 1.0),
        "mix_pre_g": 1.0 + nrm((L, D), 0.05),
        "mix_post_g": 1.0 + nrm((L, D), 0.05),
        "ffn_pre_g": 1.0 + nrm((L, D), 0.05),
        "ffn_post_g": 1.0 + nrm((L, D), 0.05),
        "w_in": nrm((L, D, N_IN), D ** -0.5),
        "mu_rkv": uni((L, 3 * D_A), 0.0, 1.0),
        "mu_wag": uni((L, 3, D), 0.0, 1.0),
        "w0": uni((L, D_A), -3.0, 1.0),
        "w1": nrm((L, D, D_DECAY_LORA), D ** -0.5),
        "w2": nrm((L, D_DECAY_LORA, D_A), 0.1 * D_DECAY_LORA ** -0.5),
        "a0": nrm((L, D_A), 0.5),
        "a1": nrm((L, D, D_AAA_LORA), D ** -0.5),
        "a2": nrm((L, D_AAA_LORA, D_A), 0.3 * D_AAA_LORA ** -0.5),
        "g1": nrm((L, D, D_GATE_LORA), D ** -0.5),
        "g2": nrm((L, D_GATE_LORA, D_A), D_GATE_LORA ** -0.5),
        "k_k": 0.85 + nrm((L, D_A), 0.05),
        "k_a": 1.0 + nrm((L, D_A), 0.05),
        "r_k": nrm((L, N_HEADS_A, HEAD_A), 0.1),
        "lnx_g": 1.0 + nrm((L, D_A), 0.05),
        "lnx_b": nrm((L, D_A), 0.02),
        "vn_g": 1.0 + nrm((L, D_B), 0.05),
        "vn_b": nrm((L, D_B), 0.02),
        "w_s": nrm((L, N_GROUPS_B, CHUNK, CHUNK), 0.5 * CHUNK ** -0.5),
        "b_s": 1.0 + nrm((L, N_GROUPS_B, CHUNK), 0.1),
        "w_out": nrm((L, D, D), D ** -0.5),
        "w_up": nrm((L, D, D_FF), D ** -0.5),
        "w_down": nrm((L, D_FF, D), D_FF ** -0.5),
        "w_pe": nrm((L, D_PLE, D), D_PLE ** -0.5),
        "w_pg": nrm((L, D, D), D ** -0.5),
    }


def reference(x_prompt, x_sample, state_wkv, state_shift, p_prompt, p_sample,
              mix_pre_g, mix_post_g, ffn_pre_g, ffn_post_g, w_in, mu_rkv, mu_wag,
              w0, w1, w2, a0, a1, a2, g1, g2, k_k, k_a, r_k, lnx_g, lnx_b,
              vn_g, vn_b, w_s, b_s, w_out, w_up, w_down, w_pe, w_pg):
    bp = x_prompt.shape[0]
    yp, ys = x_prompt, x_sample
    wkv_p, sh_p, wkv_s, sh_s, cv_s = [], [], [], [], []
    for i in range(DEPTH):
        lw = (mix_pre_g[i], mix_post_g[i], ffn_pre_g[i], ffn_post_g[i], w_in[i], mu_rkv[i],
              mu_wag[i], w0[i], w1[i], w2[i], a0[i], a1[i], a2[i], g1[i], g2[i], k_k[i],
              k_a[i], r_k[i], lnx_g[i], lnx_b[i], vn_g[i], vn_b[i], w_s[i], b_s[i],
              w_out[i], w_up[i], w_down[i], w_pe[i], w_pg[i])
        yp, S_p, last_p, _ = _layer(
            yp, p_prompt[i], jnp.zeros((bp, N_HEADS_A, HEAD_A, HEAD_A), jnp.float32),
            jnp.zeros((bp, D_MODEL), x_prompt.dtype), *lw)
        ys, S_s, last_s, vn_s = _layer(ys, p_sample[i], state_wkv[i], state_shift[i], *lw)
        wkv_p.append(S_p.astype(state_wkv.dtype))
        sh_p.append(last_p.astype(state_shift.dtype))
        wkv_s.append(S_s.astype(state_wkv.dtype))
        sh_s.append(last_s.astype(state_shift.dtype))
        cv_s.append(vn_s)
    new_wkv_prompt = jnp.stack(wkv_p)
    new_shift_prompt = jnp.stack(sh_p)
    new_wkv_sample = jnp.stack(wkv_s)
    new_shift_sample = jnp.stack(sh_s)
    new_chunk_v_sample = jnp.stack(cv_s)
    return (yp, ys, new_wkv_prompt, new_shift_prompt, new_wkv_sample, new_shift_sample, new_chunk_v_sample)
```

```python
import functools

import numpy as np
import jax
import jax.numpy as jnp
from jax import lax
from jax.experimental import pallas as pl
from jax.experimental.pallas import tpu as pltpu

F32 = jnp.float32
BF16 = jnp.bfloat16

HEAD = 64
CHUNK = 128
N_GROUPS_B = 8
EPS = 1e-6
GN_EPS = 64e-5
SQRT_HALF = float(np.sqrt(0.5))

LANES = 128
SUBLANES = 8
BF16_ROWS = 16
VMEM_LIMIT = 56 * 1024 * 1024


def _const_spec(shape):
    nd = len(shape)
    return pl.BlockSpec(shape, lambda *_: (0,) * nd, pipeline_mode=pl.Buffered(1))


def _rms(x, g):
    ms = jnp.mean(x * x, axis=-1, keepdims=True)
    return x * lax.rsqrt(ms + EPS) * g


def _gelu(x):
    return 0.5 * x * (1.0 + lax.erf(x * SQRT_HALF))


def _dot(a, b):
    return jnp.dot(a, b, preferred_element_type=F32)


def _mix_a_body(h_ref, hp_ref, init_ref, gpre_ref, wrkv_ref, wga_ref, mu_rkv_ref, mu_wag_ref,
                w0_ref, w1_ref, w2_ref, a0_ref, a1_ref, a2_ref, g1_ref, g2_ref,
                r_ref, k_ref, v_ref, w_ref, a_ref, ga_ref, last_ref, *, tm, P, d):
    D = h_ref.shape[-1]
    i = pl.program_id(0)
    g = gpre_ref[...]
    xn = _rms(h_ref[...], g)
    prev = jnp.where(i == 0, init_ref[...], _rms(hp_ref[...], g))
    xn_ext = jnp.concatenate([prev, xn], axis=0)
    xs = xn_ext[P - d:P - d + tm]
    dx = xs - xn
    mw = mu_wag_ref[...]
    xw = xn + dx * mw[0:1]
    xa = xn + dx * mw[1:2]
    xg = xn + dx * mw[2:3]
    xb_ext = xn_ext.astype(BF16)
    xb = xb_ext[P:]
    mu = mu_rkv_ref[...]
    for s, out in enumerate((r_ref, k_ref, v_ref)):
        pe = _dot(xb_ext, wrkv_ref[:, s * D:(s + 1) * D])
        cur = pe[P:]
        prv = pe[P - d:P - d + tm]
        out[...] = cur + (prv - cur) * mu[:, s * D:(s + 1) * D]
    wl = w0_ref[...] + _dot(jnp.tanh(_dot(xw.astype(BF16), w1_ref[...])).astype(BF16), w2_ref[...])
    w_log = -jax.nn.softplus(-wl) - 0.5
    w_ref[...] = jnp.exp(-jnp.exp(w_log))
    al = a0_ref[...] + _dot(_dot(xa.astype(BF16), a1_ref[...]).astype(BF16), a2_ref[...])
    a_ref[...] = jax.nn.sigmoid(al)
    gg = _dot(jax.nn.sigmoid(_dot(xg.astype(BF16), g1_ref[...])).astype(BF16), g2_ref[...])
    ga = _dot(xb, wga_ref[...])
    ga_ref[...] = jax.nn.sigmoid(ga) * gg
    last_ref[...] = xn[tm - P:]


def _mix_a(h, init_xn, lw, *, tm, P, d):
    T, D = h.shape
    n_tiles = T // tm
    ratio = tm // P
    tok = pl.BlockSpec((tm, D), lambda i: (i, 0))
    in_specs = [
        tok,
        pl.BlockSpec((P, D), lambda i: (jnp.maximum(i * ratio - 1, 0), 0)),
        _const_spec((P, D)),
        _const_spec((1, D)),
        pl.BlockSpec((D, 3 * D), lambda i: (0, 0), pipeline_mode=pl.Buffered(1)),
        pl.BlockSpec((D, D), lambda i: (0, 5), pipeline_mode=pl.Buffered(1)),
        _const_spec((1, 3 * D)), _const_spec((3, D)),
        _const_spec((1, D)), _const_spec(lw["w1"].shape), _const_spec(lw["w2"].shape),
        _const_spec((1, D)), _const_spec(lw["a1"].shape), _const_spec(lw["a2"].shape),
        _const_spec(lw["g1"].shape), _const_spec(lw["g2"].shape),
    ]
    out_specs = [tok] * 6 + [pl.BlockSpec((P, D), lambda i: (0, 0))]
    out_shape = [jax.ShapeDtypeStruct((T, D), F32)] * 6 + [jax.ShapeDtypeStruct((P, D), F32)]
    return pl.pallas_call(
        functools.partial(_mix_a_body, tm=tm, P=P, d=d),
        grid=(n_tiles,), in_specs=in_specs, out_specs=out_specs, out_shape=out_shape,
        compiler_params=pltpu.CompilerParams(dimension_semantics=("arbitrary",), vmem_limit_bytes=VMEM_LIMIT),
        name="mix_a",
    )(h, h, init_xn, lw["mix_pre_g"], lw["w_in"], lw["w_in"], lw["mu_rkv"], lw["mu_wag"],
      lw["w0"], lw["w1"], lw["w2"], lw["a0"], lw["a1"], lw["a2"], lw["g1"], lw["g2"])


def _mix_b_tokens(h, gpre, wu, wvg, wgb, vng, vnb):
    xb = _rms(h, gpre).astype(BF16)
    u = _gelu(_dot(xb, wu))
    gv = _gelu(_dot(xb, wvg))
    mean = jnp.mean(gv, axis=-1, keepdims=True)
    cen = gv - mean
    var = jnp.mean(cen * cen, axis=-1, keepdims=True)
    vn = cen * lax.rsqrt(var + EPS) * vng + vnb
    su = jax.nn.sigmoid(_dot(xb, wgb)) * u
    return su, vn


def _mix_b_prompt_body(h_ref, gpre_ref, wu_ref, wvg_ref, wgb_ref, vng_ref, vnb_ref, ws_ref, bias_ref,
                       ob_ref, vn_scr, su_scr, cm_scr, *, tm, nb):
    s = pl.program_id(1)
    n_sub = pl.num_programs(1)
    su, vn = _mix_b_tokens(h_ref[...], gpre_ref[...], wu_ref[...], wvg_ref[...], wgb_ref[...],
                           vng_ref[...], vnb_ref[...])
    row0 = pl.multiple_of(s * tm, tm)
    for g in range(N_GROUPS_B):
        vn_scr[g, pl.ds(row0, tm), :] = vn[:, g * CHUNK:(g + 1) * CHUNK]
        su_scr[g, pl.ds(row0, tm), :] = su[:, g * CHUNK:(g + 1) * CHUNK]

    @pl.when(s == n_sub - 1)
    def _():
        row = lax.broadcasted_iota(jnp.int32, (CHUNK, CHUNK), 0)
        col = lax.broadcasted_iota(jnp.int32, (CHUNK, CHUNK), 1)
        for g in range(N_GROUPS_B):
            wm = jnp.where(row >= col, ws_ref[g], 0.0).astype(BF16)
            bias = bias_ref[:, g * CHUNK:(g + 1) * CHUNK]
            for b in range(nb):
                rows = pl.ds(b, CHUNK, stride=nb)
                mixed = _dot(wm, vn_scr[g, rows, :].astype(BF16)) + bias
                cm_scr[g, rows, :] = mixed * su_scr[g, rows, :]
        for g in range(N_GROUPS_B):
            ob_ref[:, g * CHUNK:(g + 1) * CHUNK] = cm_scr[g]


def _mix_b_prompt(h, lw, *, nb, tm):
    T, D = h.shape
    rows_per_chunk = CHUNK * nb
    n_chunks = T // rows_per_chunk
    n_sub = rows_per_chunk // tm
    wcol = lambda j: pl.BlockSpec((D, D), lambda c, s: (0, j), pipeline_mode=pl.Buffered(1))
    in_specs = [
        pl.BlockSpec((tm, D), lambda c, s: (c * n_sub + s, 0)),
        _const_spec((1, D)), wcol(3), wcol(4), wcol(6),
        _const_spec((1, D)), _const_spec((1, D)),
        _const_spec((N_GROUPS_B, CHUNK, CHUNK)), _const_spec((CHUNK, D)),
    ]
    slab = pltpu.VMEM((N_GROUPS_B, rows_per_chunk, CHUNK), F32)
    return pl.pallas_call(
        functools.partial(_mix_b_prompt_body, tm=tm, nb=nb),
        grid=(n_chunks, n_sub), in_specs=in_specs,
        out_specs=pl.BlockSpec((rows_per_chunk, D), lambda c, s: (c, 0)),
        out_shape=jax.ShapeDtypeStruct((T, D), F32),
        scratch_shapes=[slab, slab, slab],
        compiler_params=pltpu.CompilerParams(dimension_semantics=("arbitrary", "arbitrary"),
                                             vmem_limit_bytes=VMEM_LIMIT),
        name="mix_b_prompt",
    )(h, lw["mix_pre_g"], lw["w_in"], lw["w_in"], lw["w_in"], lw["vn_g"], lw["vn_b"], lw["w_s"], lw["bias_full"])


def _mix_b_sample_body(ws_ref, bs_ref, h_ref, gpre_ref, wu_ref, wvg_ref, wgb_ref, vng_ref, vnb_ref,
                       ob_ref, vn_ref, *, nb, L):
    su, vn = _mix_b_tokens(h_ref[...], gpre_ref[...], wu_ref[...], wvg_ref[...], wgb_ref[...],
                           vng_ref[...], vnb_ref[...])
    vn_ref[...] = vn
    for t in range(L):
        cols = []
        for g in range(N_GROUPS_B):
            acc = jnp.full((nb, CHUNK), bs_ref[g * L + t], F32)
            for s in range(t + 1):
                acc = acc + ws_ref[(g * L + t) * L + s] * vn[s * nb:(s + 1) * nb, g * CHUNK:(g + 1) * CHUNK]
            cols.append(acc)
        ob_ref[t * nb:(t + 1) * nb, :] = jnp.concatenate(cols, axis=1) * su[t * nb:(t + 1) * nb]


def _mix_b_sample(h, lw, *, nb, L):
    T, D = h.shape
    wcol = lambda j: pl.BlockSpec((D, D), lambda i: (0, j), pipeline_mode=pl.Buffered(1))
    smem = pl.BlockSpec(memory_space=pltpu.SMEM)
    in_specs = [smem, smem, _const_spec((T, D)), _const_spec((1, D)), wcol(3), wcol(4), wcol(6),
                _const_spec((1, D)), _const_spec((1, D))]
    full = pl.BlockSpec((T, D), lambda i: (0, 0))
    return pl.pallas_call(
        functools.partial(_mix_b_sample_body, nb=nb, L=L),
        grid=(1,), in_specs=in_specs, out_specs=[full, full],
        out_shape=[jax.ShapeDtypeStruct((T, D), F32)] * 2,
        compiler_params=pltpu.CompilerParams(dimension_semantics=("arbitrary",), vmem_limit_bytes=VMEM_LIMIT),
        name="mix_b_sample",
    )(lw["ws_head"], lw["bs_head"], h, lw["mix_pre_g"], lw["w_in"], lw["w_in"], lw["w_in"], lw["vn_g"], lw["vn_b"])


def _wkv_body(r_ref, k_ref, v_ref, w_ref, a_ref, s0_ref, kk_ref, ka_ref, rk_ref, lg_ref, lb_ref,
              o_ref, st_ref, vec_ref, *, TT, n_heads):
    @pl.when(pl.program_id(1) == 0)
    def _():
        st_ref[...] = s0_ref[...]

    def to_lanes(x):
        return jnp.concatenate([x[:, h * HEAD:(h + 1) * HEAD] for h in range(n_heads)], axis=0).T

    def from_lanes(y):
        z = y.T
        return jnp.concatenate([z[h * SUBLANES:(h + 1) * SUBLANES, :] for h in range(n_heads)], axis=1)

    def row(idx, j):
        return jnp.broadcast_to(vec_ref[idx, j:j + 1, :], (HEAD, LANES))

    def step(tt, carry):
        r = to_lanes(r_ref[tt])
        k = to_lanes(k_ref[tt])
        v = to_lanes(v_ref[tt])
        w = to_lanes(w_ref[tt])
        a = to_lanes(a_ref[tt])
        kk = k * kk_ref[...]
        nrm = jnp.sqrt(jnp.sum(kk * kk, axis=0, keepdims=True))
        kk = kk / jnp.maximum(nrm, 1e-12)
        k2 = k * (1.0 + (a - 1.0) * ka_ref[...])
        vec_ref[0] = -kk
        vec_ref[1] = w
        vec_ref[2] = kk * a
        vec_ref[3] = k2
        vec_ref[4] = r
        sa = jnp.zeros((HEAD, LANES), F32)
        for j in range(HEAD):
            sa = sa + st_ref[j] * row(0, j)
        y = jnp.zeros((HEAD, LANES), F32)
        for j in range(HEAD):
            sn = st_ref[j] * row(1, j) + sa * row(2, j) + v * row(3, j)
            st_ref[j] = sn
            y = y + sn * row(4, j)
        mean = jnp.mean(y, axis=0, keepdims=True)
        cen = y - mean
        var = jnp.mean(cen * cen, axis=0, keepdims=True)
        yn = cen * lax.rsqrt(var + GN_EPS) * lg_ref[...] + lb_ref[...]
        bonus = jnp.sum(r * k2 * rk_ref[...], axis=0, keepdims=True)
        o_ref[tt] = from_lanes(yn + bonus * v)
        return carry

    lax.fori_loop(0, TT, step, 0)


def _wkv(r, k, v, w, a, s0, lw, *, L, nb, TT):
    T, D = r.shape
    G = nb // SUBLANES
    n_heads = D // HEAD
    view = lambda x: x.reshape(L, G, SUBLANES, D)
    tok = pl.BlockSpec((TT, None, SUBLANES, D), lambda g, t: (t, g, 0, 0))
    st = pl.BlockSpec((None, HEAD, HEAD, LANES), lambda g, t: (g, 0, 0, 0))
    par = pl.BlockSpec((HEAD, LANES), lambda g, t: (0, 0))
    o, sT = pl.pallas_call(
        functools.partial(_wkv_body, TT=TT, n_heads=n_heads),
        grid=(G, L // TT),
        in_specs=[tok] * 5 + [st] + [par] * 5,
        out_specs=[tok, st],
        out_shape=[jax.ShapeDtypeStruct((L, G, SUBLANES, D), F32),
                   jax.ShapeDtypeStruct((G, HEAD, HEAD, LANES), F32)],
        scratch_shapes=[pltpu.VMEM((5, HEAD, LANES), F32)],
        compiler_params=pltpu.CompilerParams(dimension_semantics=("arbitrary", "arbitrary"),
                                             vmem_limit_bytes=VMEM_LIMIT),
        name="wkv",
    )(view(r), view(k), view(v), view(w), view(a), s0,
      lw["kk_l"], lw["ka_l"], lw["rk_l"], lw["lg_l"], lw["lb_l"])
    return o.reshape(T, D), sT


def _post_body(h_ref, o_ref, ga_ref, ob_ref, p_ref, gpost_ref, wout_ref, gfpre_ref, wup_ref, wdown_ref,
               gfpost_ref, wpg_ref, wpe_ref, out_ref):
    mixed = ga_ref[...] * o_ref[...] + ob_ref[...]
    h = h_ref[...] + _rms(_dot(mixed.astype(BF16), wout_ref[...]), gpost_ref[...])
    z = _dot(_rms(h, gfpre_ref[...]).astype(BF16), wup_ref[...])
    z = jnp.square(jnp.maximum(z, 0.0))
    h = h + _rms(_dot(z.astype(BF16), wdown_ref[...]), gfpost_ref[...])
    gate = jax.nn.sigmoid(_dot(h.astype(BF16), wpg_ref[...]))
    out_ref[...] = h + gate * _dot(p_ref[...].astype(BF16), wpe_ref[...])


def _post(h, o, ga, ob, p, lw, *, tm):
    T, D = h.shape
    tok = pl.BlockSpec((tm, D), lambda i: (i, 0))
    in_specs = [tok, tok, tok, tok, pl.BlockSpec((tm, p.shape[1]), lambda i: (i, 0)),
                _const_spec((1, D)), _const_spec(lw["w_out"].shape),
                _const_spec((1, D)), _const_spec(lw["w_up"].shape), _const_spec(lw["w_down"].shape),
                _const_spec((1, D)), _const_spec(lw["w_pg"].shape), _const_spec(lw["w_pe"].shape)]
    return pl.pallas_call(
        _post_body, grid=(T // tm,), in_specs=in_specs, out_specs=tok,
        out_shape=jax.ShapeDtypeStruct((T, D), F32),
        compiler_params=pltpu.CompilerParams(dimension_semantics=("arbitrary",), vmem_limit_bytes=VMEM_LIMIT),
        name="post",
    )(h, o, ga, ob, p, lw["mix_post_g"], lw["w_out"], lw["ffn_pre_g"], lw["w_up"], lw["w_down"],
      lw["ffn_post_g"], lw["w_pg"], lw["w_pe"])


def _lane_param(x, n_heads):
    return jnp.repeat(x.reshape(n_heads, HEAD).T, SUBLANES, axis=1).astype(F32)


def _state_to_lanes(s):
    nb, nh = s.shape[0], s.shape[1]
    s = s.reshape(nb // SUBLANES, SUBLANES, nh, HEAD, HEAD)
    return jnp.transpose(s, (0, 4, 3, 2, 1)).reshape(nb // SUBLANES, HEAD, HEAD, nh * SUBLANES)


def _state_from_lanes(s, nh):
    G = s.shape[0]
    s = s.reshape(G, HEAD, HEAD, nh, SUBLANES)
    return jnp.transpose(s, (0, 4, 3, 2, 1)).reshape(G * SUBLANES, nh, HEAD, HEAD)


def _layer(h, p, s0, init_xn, lw, *, L, nb, prompt):
    D = h.shape[1]
    nh = D // HEAD
    if prompt:
        tm, P = 256, BF16_ROWS
        r, k, v, w, a, ga, last = _mix_a(h, init_xn, lw, tm=tm, P=P, d=nb)
        ob = _mix_b_prompt(h, lw, nb=nb, tm=256)
        vn = None
        TT = 32
    else:
        r, k, v, w, a, ga, last = _mix_a(h, init_xn, lw, tm=nb, P=nb, d=nb)
        ob, vn = _mix_b_sample(h, lw, nb=nb, L=L)
        TT = L
    o, sT = _wkv(r, k, v, w, a, s0, lw, L=L, nb=nb, TT=TT)
    h = _post(h, o, ga, ob, p, lw, tm=min(256, h.shape[0]))
    return h, _state_from_lanes(sT, nh), last[-nb:], vn


def kernel(x_prompt, x_sample, state_wkv, state_shift, p_prompt, p_sample, mix_pre_g, mix_post_g, ffn_pre_g, ffn_post_g, w_in, mu_rkv, mu_wag, w0, w1, w2, a0, a1, a2, g1, g2, k_k, k_a, r_k, lnx_g, lnx_b, vn_g, vn_b, w_s, b_s, w_out, w_up, w_down, w_pe, w_pg):
    B, L, D = x_prompt.shape
    Bs, Ls, _ = x_sample.shape
    depth = w_in.shape[0]
    nh = D // HEAD
    row = lambda x: x.reshape(1, -1).astype(F32)
    tmajor = lambda x: jnp.swapaxes(x, 0, 1).reshape(-1, x.shape[-1])

    hp = tmajor(x_prompt)
    hs = tmajor(x_sample)
    zero_state = jnp.zeros((B // SUBLANES, HEAD, HEAD, nh * SUBLANES), F32)
    wkv_p, sh_p, wkv_s, sh_s, cv_s = [], [], [], [], []
    for i in range(depth):
        lw = dict(
            mix_pre_g=row(mix_pre_g[i]), mix_post_g=row(mix_post_g[i]),
            ffn_pre_g=row(ffn_pre_g[i]), ffn_post_g=row(ffn_post_g[i]),
            w_in=w_in[i].astype(BF16), mu_rkv=row(mu_rkv[i]), mu_wag=mu_wag[i],
            w0=row(w0[i]), w1=w1[i].astype(BF16), w2=w2[i].astype(BF16),
            a0=row(a0[i]), a1=a1[i].astype(BF16), a2=a2[i].astype(BF16),
            g1=g1[i].astype(BF16), g2=g2[i].astype(BF16),
            kk_l=_lane_param(k_k[i], nh), ka_l=_lane_param(k_a[i], nh), rk_l=_lane_param(r_k[i].reshape(-1), nh),
            lg_l=_lane_param(lnx_g[i], nh), lb_l=_lane_param(lnx_b[i], nh),
            vn_g=row(vn_g[i]), vn_b=row(vn_b[i]), w_s=w_s[i],
            bias_full=jnp.repeat(b_s[i].T, CHUNK, axis=1),
            ws_head=w_s[i][:, :Ls, :Ls].reshape(-1), bs_head=b_s[i][:, :Ls].reshape(-1),
            w_out=w_out[i].astype(BF16), w_up=w_up[i].astype(BF16), w_down=w_down[i].astype(BF16),
            w_pe=w_pe[i].astype(BF16), w_pg=w_pg[i].astype(BF16),
        )
        hp, S_p, last_p, _ = _layer(hp, tmajor(p_prompt[i]), zero_state, jnp.zeros((BF16_ROWS, D), F32), lw,
                                    L=L, nb=B, prompt=True)
        hs, S_s, last_s, vn_s = _layer(hs, tmajor(p_sample[i]), _state_to_lanes(state_wkv[i]), state_shift[i], lw,
                                       L=Ls, nb=Bs, prompt=False)
        wkv_p.append(S_p)
        sh_p.append(last_p)
        wkv_s.append(S_s)
        sh_s.append(last_s)
        cv_s.append(jnp.swapaxes(vn_s.reshape(Ls, Bs, D), 0, 1))
    yp = jnp.swapaxes(hp.reshape(L, B, D), 0, 1)
    ys = jnp.swapaxes(hs.reshape(Ls, Bs, D), 0, 1)
    return (yp, ys, jnp.stack(wkv_p), jnp.stack(sh_p), jnp.stack(wkv_s), jnp.stack(sh_s), jnp.stack(cv_s))
```

```python
import functools

import numpy as np
import jax
import jax.numpy as jnp
from jax import lax
from jax.experimental import pallas as pl
from jax.experimental.pallas import tpu as pltpu

F32 = jnp.float32
BF16 = jnp.bfloat16

HEAD = 64
CHUNK = 128
N_GROUPS_B = 8
EPS = 1e-6
GN_EPS = 64e-5
SQRT_HALF = float(np.sqrt(0.5))

LANES = 128
SUBLANES = 8
BF16_ROWS = 16
VMEM_LIMIT = 56 * 1024 * 1024


def _const_spec(shape):
    nd = len(shape)
    return pl.BlockSpec(shape, lambda *_: (0,) * nd, pipeline_mode=pl.Buffered(1))


def _rms(x, g):
    ms = jnp.mean(x * x, axis=-1, keepdims=True)
    return x * lax.rsqrt(ms + EPS) * g


def _gelu(x):
    return 0.5 * x * (1.0 + lax.erf(x * SQRT_HALF))


def _dot(a, b):
    return jnp.dot(a, b, preferred_element_type=F32)


def _mix_a_body(h_ref, hp_ref, init_ref, gpre_ref, wrkv_ref, wga_ref, mu_rkv_ref, mu_wag_ref,
                w0_ref, w1_ref, w2_ref, a0_ref, a1_ref, a2_ref, g1_ref, g2_ref,
                r_ref, k_ref, v_ref, w_ref, a_ref, ga_ref, last_ref, *, tm, P, d):
    D = h_ref.shape[-1]
    i = pl.program_id(0)
    g = gpre_ref[...]
    xn = _rms(h_ref[...], g)
    prev = jnp.where(i == 0, init_ref[...], _rms(hp_ref[...], g))
    xn_ext = jnp.concatenate([prev, xn], axis=0)
    xs = xn_ext[P - d:P - d + tm]
    dx = xs - xn
    mw = mu_wag_ref[...]
    xw = xn + dx * mw[0:1]
    xa = xn + dx * mw[1:2]
    xg = xn + dx * mw[2:3]
    xb_ext = xn_ext.astype(BF16)
    xb = xb_ext[P:]
    mu = mu_rkv_ref[...]
    for s, out in enumerate((r_ref, k_ref, v_ref)):
        pe = _dot(xb_ext, wrkv_ref[:, s * D:(s + 1) * D])
        cur = pe[P:]
        prv = pe[P - d:P - d + tm]
        out[...] = cur + (prv - cur) * mu[:, s * D:(s + 1) * D]
    wl = w0_ref[...] + _dot(jnp.tanh(_dot(xw.astype(BF16), w1_ref[...])).astype(BF16), w2_ref[...])
    w_log = -jax.nn.softplus(-wl) - 0.5
    w_ref[...] = jnp.exp(-jnp.exp(w_log))
    al = a0_ref[...] + _dot(_dot(xa.astype(BF16), a1_ref[...]).astype(BF16), a2_ref[...])
    a_ref[...] = jax.nn.sigmoid(al)
    gg = _dot(jax.nn.sigmoid(_dot(xg.astype(BF16), g1_ref[...])).astype(BF16), g2_ref[...])
    ga = _dot(xb, wga_ref[...])
    ga_ref[...] = jax.nn.sigmoid(ga) * gg
    last_ref[...] = xn[tm - P:]


def _mix_a(h, init_xn, lw, *, tm, P, d):
    T, D = h.shape
    n_tiles = T // tm
    ratio = tm // P
    tok = pl.BlockSpec((tm, D), lambda i: (i, 0))
    in_specs = [
        tok,
        pl.BlockSpec((P, D), lambda i: (jnp.maximum(i * ratio - 1, 0), 0)),
        _const_spec((P, D)),
        _const_spec((1, D)),
        pl.BlockSpec((D, 3 * D), lambda i: (0, 0), pipeline_mode=pl.Buffered(1)),
        pl.BlockSpec((D, D), lambda i: (0, 5), pipeline_mode=pl.Buffered(1)),
        _const_spec((1, 3 * D)), _const_spec((3, D)),
        _const_spec((1, D)), _const_spec(lw["w1"].shape), _const_spec(lw["w2"].shape),
        _const_spec((1, D)), _const_spec(lw["a1"].shape), _const_spec(lw["a2"].shape),
        _const_spec(lw["g1"].shape), _const_spec(lw["g2"].shape),
    ]
    out_specs = [tok] * 6 + [pl.BlockSpec((P, D), lambda i: (0, 0))]
    out_shape = [jax.ShapeDtypeStruct((T, D), F32)] * 6 + [jax.ShapeDtypeStruct((P, D), F32)]
    return pl.pallas_call(
        functools.partial(_mix_a_body, tm=tm, P=P, d=d),
        grid=(n_tiles,), in_specs=in_specs, out_specs=out_specs, out_shape=out_shape,
        compiler_params=pltpu.CompilerParams(dimension_semantics=("arbitrary",), vmem_limit_bytes=VMEM_LIMIT),
        name="mix_a",
    )(h, h, init_xn, lw["mix_pre_g"], lw["w_in"], lw["w_in"], lw["mu_rkv"], lw["mu_wag"],
      lw["w0"], lw["w1"], lw["w2"], lw["a0"], lw["a1"], lw["a2"], lw["g1"], lw["g2"])


def _mix_b_tokens(h, gpre, wu, wvg, wgb, vng, vnb):
    xb = _rms(h, gpre).astype(BF16)
    u = _gelu(_dot(xb, wu))
    gv = _gelu(_dot(xb, wvg))
    mean = jnp.mean(gv, axis=-1, keepdims=True)
    cen = gv - mean
    var = jnp.mean(cen * cen, axis=-1, keepdims=True)
    vn = cen * lax.rsqrt(var + EPS) * vng + vnb
    su = jax.nn.sigmoid(_dot(xb, wgb)) * u
    return su, vn


def _mix_b_prompt_body(h_ref, gpre_ref, wu_ref, wvg_ref, wgb_ref, vng_ref, vnb_ref, ws_ref, bias_ref,
                       ob_ref, vn_scr, su_scr, cm_scr, *, tm, nb):
    s = pl.program_id(1)
    n_sub = pl.num_programs(1)
    su, vn = _mix_b_tokens(h_ref[...], gpre_ref[...], wu_ref[...], wvg_ref[...], wgb_ref[...],
                           vng_ref[...], vnb_ref[...])
    row0 = pl.multiple_of(s * tm, tm)
    for g in range(N_GROUPS_B):
        vn_scr[g, pl.ds(row0, tm), :] = vn[:, g * CHUNK:(g + 1) * CHUNK]
        su_scr[g, pl.ds(row0, tm), :] = su[:, g * CHUNK:(g + 1) * CHUNK]

    @pl.when(s == n_sub - 1)
    def _():
        row = lax.broadcasted_iota(jnp.int32, (CHUNK, CHUNK), 0)
        col = lax.broadcasted_iota(jnp.int32, (CHUNK, CHUNK), 1)
        for g in range(N_GROUPS_B):
            wm = jnp.where(row >= col, ws_ref[g], 0.0).astype(BF16)
            bias = bias_ref[:, g * CHUNK:(g + 1) * CHUNK]
            for b in range(nb):
                rows = pl.ds(b, CHUNK, stride=nb)
                mixed = _dot(wm, vn_scr[g, rows, :].astype(BF16)) + bias
                cm_scr[g, rows, :] = mixed * su_scr[g, rows, :]
        for g in range(N_GROUPS_B):
            ob_ref[:, g * CHUNK:(g + 1) * CHUNK] = cm_scr[g]


def _mix_b_prompt(h, lw, *, nb, tm):
    T, D = h.shape
    rows_per_chunk = CHUNK * nb
    n_chunks = T // rows_per_chunk
    n_sub = rows_per_chunk // tm
    wcol = lambda j: pl.BlockSpec((D, D), lambda c, s: (0, j), pipeline_mode=pl.Buffered(1))
    in_specs = [
        pl.BlockSpec((tm, D), lambda c, s: (c * n_sub + s, 0)),
        _const_spec((1, D)), wcol(3), wcol(4), wcol(6),
        _const_spec((1, D)), _const_spec((1, D)),
        _const_spec((N_GROUPS_B, CHUNK, CHUNK)), _const_spec((CHUNK, D)),
    ]
    slab = pltpu.VMEM((N_GROUPS_B, rows_per_chunk, CHUNK), F32)
    return pl.pallas_call(
        functools.partial(_mix_b_prompt_body, tm=tm, nb=nb),
        grid=(n_chunks, n_sub), in_specs=in_specs,
        out_specs=pl.BlockSpec((rows_per_chunk, D), lambda c, s: (c, 0)),
        out_shape=jax.ShapeDtypeStruct((T, D), F32),
        scratch_shapes=[slab, slab, slab],
        compiler_params=pltpu.CompilerParams(dimension_semantics=("arbitrary", "arbitrary"),
                                             vmem_limit_bytes=VMEM_LIMIT),
        name="mix_b_prompt",
    )(h, lw["mix_pre_g"], lw["w_in"], lw["w_in"], lw["w_in"], lw["vn_g"], lw["vn_b"], lw["w_s"], lw["bias_full"])


def _mix_b_sample_body(ws_ref, bs_ref, h_ref, gpre_ref, wu_ref, wvg_ref, wgb_ref, vng_ref, vnb_ref,
                       ob_ref, vn_ref, *, nb, L):
    su, vn = _mix_b_tokens(h_ref[...], gpre_ref[...], wu_ref[...], wvg_ref[...], wgb_ref[...],
                           vng_ref[...], vnb_ref[...])
    vn_ref[...] = vn
    for t in range(L):
        cols = []
        for g in range(N_GROUPS_B):
            acc = jnp.full((nb, CHUNK), bs_ref[g * L + t], F32)
            for s in range(t + 1):
                acc = acc + ws_ref[(g * L + t) * L + s] * vn[s * nb:(s + 1) * nb, g * CHUNK:(g + 1) * CHUNK]
            cols.append(acc)
        ob_ref[t * nb:(t + 1) * nb, :] = jnp.concatenate(cols, axis=1) * su[t * nb:(t + 1) * nb]


def _mix_b_sample(h, lw, *, nb, L):
    T, D = h.shape
    wcol = lambda j: pl.BlockSpec((D, D), lambda i: (0, j), pipeline_mode=pl.Buffered(1))
    smem = pl.BlockSpec(memory_space=pltpu.SMEM)
    in_specs = [smem, smem, _const_spec((T, D)), _const_spec((1, D)), wcol(3), wcol(4), wcol(6),
                _const_spec((1, D)), _const_spec((1, D))]
    full = pl.BlockSpec((T, D), lambda i: (0, 0))
    return pl.pallas_call(
        functools.partial(_mix_b_sample_body, nb=nb, L=L),
        grid=(1,), in_specs=in_specs, out_specs=[full, full],
        out_shape=[jax.ShapeDtypeStruct((T, D), F32)] * 2,
        compiler_params=pltpu.CompilerParams(dimension_semantics=("arbitrary",), vmem_limit_bytes=VMEM_LIMIT),
        name="mix_b_sample",
    )(lw["ws_head"], lw["bs_head"], h, lw["mix_pre_g"], lw["w_in"], lw["w_in"], lw["w_in"], lw["vn_g"], lw["vn_b"])


def _wkv_body(r_ref, k_ref, v_ref, w_ref, a_ref, s0_ref, kk_ref, ka_ref, rk_ref, lg_ref, lb_ref,
              o_ref, st_ref, vec_ref, y_ref, *, TT, n_heads):
    NA, W, BB, K2, R, V = range(6)
    HALF = HEAD // 2

    @pl.when(pl.program_id(1) == 0)
    def _():
        st_ref[...] = s0_ref[...]

    def to_lanes(x):
        return jnp.concatenate([x[:, h * HEAD:(h + 1) * HEAD] for h in range(n_heads)], axis=0).T

    def from_lanes(y):
        z = y.T
        return jnp.concatenate([z[h * SUBLANES:(h + 1) * SUBLANES, :] for h in range(n_heads)], axis=1)

    def stage(tt, carry):
        k = to_lanes(k_ref[tt])
        a = to_lanes(a_ref[tt])
        kk = k * kk_ref[...]
        nrm = jnp.sqrt(jnp.sum(kk * kk, axis=0, keepdims=True))
        kk = kk / jnp.maximum(nrm, 1e-12)
        vec_ref[tt, NA] = -kk
        vec_ref[tt, W] = to_lanes(w_ref[tt])
        vec_ref[tt, BB] = kk * a
        vec_ref[tt, K2] = k * (1.0 + (a - 1.0) * ka_ref[...])
        vec_ref[tt, R] = to_lanes(r_ref[tt])
        vec_ref[tt, V] = to_lanes(v_ref[tt])
        return carry

    lax.fori_loop(0, TT, stage, 0, unroll=2)

    sa0 = jnp.zeros((HEAD, LANES), F32)
    for j in range(HEAD):
        sa0 = sa0 + st_ref[j] * vec_ref[0, NA, j:j + 1, :]

    def step(tt, sa):
        tn = jnp.minimum(tt + 1, TT - 1)
        sa_next = []
        for half in range(2):
            rows = slice(half * HALF, (half + 1) * HALF)
            sa_h = sa[rows]
            v_h = vec_ref[tt, V, rows, :]
            y_h = jnp.zeros((HALF, LANES), F32)
            san_h = jnp.zeros((HALF, LANES), F32)
            for j in range(HEAD):
                jr = slice(j, j + 1)
                sn = (st_ref[j, rows, :] * vec_ref[tt, W, jr, :] + sa_h * vec_ref[tt, BB, jr, :]
                      + v_h * vec_ref[tt, K2, jr, :])
                st_ref[j, rows, :] = sn
                y_h = y_h + sn * vec_ref[tt, R, jr, :]
                san_h = san_h + sn * vec_ref[tn, NA, jr, :]
            y_ref[tt, rows, :] = y_h
            sa_next.append(san_h)
        return jnp.concatenate(sa_next, axis=0)

    lax.fori_loop(0, TT, step, sa0)

    def finish(tt, carry):
        y = y_ref[tt]
        mean = jnp.mean(y, axis=0, keepdims=True)
        cen = y - mean
        var = jnp.mean(cen * cen, axis=0, keepdims=True)
        yn = cen * lax.rsqrt(var + GN_EPS) * lg_ref[...] + lb_ref[...]
        bonus = jnp.sum(vec_ref[tt, R] * vec_ref[tt, K2] * rk_ref[...], axis=0, keepdims=True)
        o_ref[tt] = from_lanes(yn + bonus * vec_ref[tt, V])
        return carry

    lax.fori_loop(0, TT, finish, 0, unroll=2)


def _wkv(r, k, v, w, a, s0, lw, *, L, nb, TT):
    T, D = r.shape
    G = nb // SUBLANES
    n_heads = D // HEAD
    view = lambda x: x.reshape(L, G, SUBLANES, D)
    tok = pl.BlockSpec((TT, None, SUBLANES, D), lambda g, t: (t, g, 0, 0))
    st = pl.BlockSpec((None, HEAD, HEAD, LANES), lambda g, t: (g, 0, 0, 0))
    par = pl.BlockSpec((HEAD, LANES), lambda g, t: (0, 0))
    o, sT = pl.pallas_call(
        functools.partial(_wkv_body, TT=TT, n_heads=n_heads),
        grid=(G, L // TT),
        in_specs=[tok] * 5 + [st] + [par] * 5,
        out_specs=[tok, st],
        out_shape=[jax.ShapeDtypeStruct((L, G, SUBLANES, D), F32),
                   jax.ShapeDtypeStruct((G, HEAD, HEAD, LANES), F32)],
        scratch_shapes=[pltpu.VMEM((TT, 6, HEAD, LANES), F32), pltpu.VMEM((TT, HEAD, LANES), F32)],
        compiler_params=pltpu.CompilerParams(dimension_semantics=("arbitrary", "arbitrary"),
                                             vmem_limit_bytes=VMEM_LIMIT),
        name="wkv",
    )(view(r), view(k), view(v), view(w), view(a), s0,
      lw["kk_l"], lw["ka_l"], lw["rk_l"], lw["lg_l"], lw["lb_l"])
    return o.reshape(T, D), sT


def _post_body(h_ref, o_ref, ga_ref, ob_ref, p_ref, gpost_ref, wout_ref, gfpre_ref, wup_ref, wdown_ref,
               gfpost_ref, wpg_ref, wpe_ref, out_ref):
    mixed = ga_ref[...] * o_ref[...] + ob_ref[...]
    h = h_ref[...] + _rms(_dot(mixed.astype(BF16), wout_ref[...]), gpost_ref[...])
    z = _dot(_rms(h, gfpre_ref[...]).astype(BF16), wup_ref[...])
    z = jnp.square(jnp.maximum(z, 0.0))
    h = h + _rms(_dot(z.astype(BF16), wdown_ref[...]), gfpost_ref[...])
    gate = jax.nn.sigmoid(_dot(h.astype(BF16), wpg_ref[...]))
    out_ref[...] = h + gate * _dot(p_ref[...].astype(BF16), wpe_ref[...])


def _post(h, o, ga, ob, p, lw, *, tm):
    T, D = h.shape
    tok = pl.BlockSpec((tm, D), lambda i: (i, 0))
    in_specs = [tok, tok, tok, tok, pl.BlockSpec((tm, p.shape[1]), lambda i: (i, 0)),
                _const_spec((1, D)), _const_spec(lw["w_out"].shape),
                _const_spec((1, D)), _const_spec(lw["w_up"].shape), _const_spec(lw["w_down"].shape),
                _const_spec((1, D)), _const_spec(lw["w_pg"].shape), _const_spec(lw["w_pe"].shape)]
    return pl.pallas_call(
        _post_body, grid=(T // tm,), in_specs=in_specs, out_specs=tok,
        out_shape=jax.ShapeDtypeStruct((T, D), F32),
        compiler_params=pltpu.CompilerParams(dimension_semantics=("arbitrary",), vmem_limit_bytes=VMEM_LIMIT),
        name="post",
    )(h, o, ga, ob, p, lw["mix_post_g"], lw["w_out"], lw["ffn_pre_g"], lw["w_up"], lw["w_down"],
      lw["ffn_post_g"], lw["w_pg"], lw["w_pe"])


def _lane_param(x, n_heads):
    return jnp.repeat(x.reshape(n_heads, HEAD).T, SUBLANES, axis=1).astype(F32)


def _state_to_lanes(s):
    nb, nh = s.shape[0], s.shape[1]
    s = s.reshape(nb // SUBLANES, SUBLANES, nh, HEAD, HEAD)
    return jnp.transpose(s, (0, 4, 3, 2, 1)).reshape(nb // SUBLANES, HEAD, HEAD, nh * SUBLANES)


def _state_from_lanes(s, nh):
    G = s.shape[0]
    s = s.reshape(G, HEAD, HEAD, nh, SUBLANES)
    return jnp.transpose(s, (0, 4, 3, 2, 1)).reshape(G * SUBLANES, nh, HEAD, HEAD)


def _layer(h, p, s0, init_xn, lw, *, L, nb, prompt):
    D = h.shape[1]
    nh = D // HEAD
    if prompt:
        tm, P = 256, BF16_ROWS
        r, k, v, w, a, ga, last = _mix_a(h, init_xn, lw, tm=tm, P=P, d=nb)
        ob = _mix_b_prompt(h, lw, nb=nb, tm=256)
        vn = None
        TT = 32
    else:
        r, k, v, w, a, ga, last = _mix_a(h, init_xn, lw, tm=nb, P=nb, d=nb)
        ob, vn = _mix_b_sample(h, lw, nb=nb, L=L)
        TT = L
    o, sT = _wkv(r, k, v, w, a, s0, lw, L=L, nb=nb, TT=TT)
    h = _post(h, o, ga, ob, p, lw, tm=min(256, h.shape[0]))
    return h, _state_from_lanes(sT, nh), last[-nb:], vn


def kernel(x_prompt, x_sample, state_wkv, state_shift, p_prompt, p_sample, mix_pre_g, mix_post_g, ffn_pre_g, ffn_post_g, w_in, mu_rkv, mu_wag, w0, w1, w2, a0, a1, a2, g1, g2, k_k, k_a, r_k, lnx_g, lnx_b, vn_g, vn_b, w_s, b_s, w_out, w_up, w_down, w_pe, w_pg):
    B, L, D = x_prompt.shape
    Bs, Ls, _ = x_sample.shape
    depth = w_in.shape[0]
    nh = D // HEAD
    row = lambda x: x.reshape(1, -1).astype(F32)
    tmajor = lambda x: jnp.swapaxes(x, 0, 1).reshape(-1, x.shape[-1])

    hp = tmajor(x_prompt)
    hs = tmajor(x_sample)
    zero_state = jnp.zeros((B // SUBLANES, HEAD, HEAD, nh * SUBLANES), F32)
    wkv_p, sh_p, wkv_s, sh_s, cv_s = [], [], [], [], []
    for i in range(depth):
        lw = dict(
            mix_pre_g=row(mix_pre_g[i]), mix_post_g=row(mix_post_g[i]),
            ffn_pre_g=row(ffn_pre_g[i]), ffn_post_g=row(ffn_post_g[i]),
            w_in=w_in[i].astype(BF16), mu_rkv=row(mu_rkv[i]), mu_wag=mu_wag[i],
            w0=row(w0[i]), w1=w1[i].astype(BF16), w2=w2[i].astype(BF16),
            a0=row(a0[i]), a1=a1[i].astype(BF16), a2=a2[i].astype(BF16),
            g1=g1[i].astype(BF16), g2=g2[i].astype(BF16),
            kk_l=_lane_param(k_k[i], nh), ka_l=_lane_param(k_a[i], nh), rk_l=_lane_param(r_k[i].reshape(-1), nh),
            lg_l=_lane_param(lnx_g[i], nh), lb_l=_lane_param(lnx_b[i], nh),
            vn_g=row(vn_g[i]), vn_b=row(vn_b[i]), w_s=w_s[i],
            bias_full=jnp.repeat(b_s[i].T, CHUNK, axis=1),
            ws_head=w_s[i][:, :Ls, :Ls].reshape(-1), bs_head=b_s[i][:, :Ls].reshape(-1),
            w_out=w_out[i].astype(BF16), w_up=w_up[i].astype(BF16), w_down=w_down[i].astype(BF16),
            w_pe=w_pe[i].astype(BF16), w_pg=w_pg[i].astype(BF16),
        )
        hp, S_p, last_p, _ = _layer(hp, tmajor(p_prompt[i]), zero_state, jnp.zeros((BF16_ROWS, D), F32), lw,
                                    L=L, nb=B, prompt=True)
        hs, S_s, last_s, vn_s = _layer(hs, tmajor(p_sample[i]), _state_to_lanes(state_wkv[i]), state_shift[i], lw,
                                       L=Ls, nb=Bs, prompt=False)
        wkv_p.append(S_p)
        sh_p.append(last_p)
        wkv_s.append(S_s)
        sh_s.append(last_s)
        cv_s.append(jnp.swapaxes(vn_s.reshape(Ls, Bs, D), 0, 1))
    yp = jnp.swapaxes(hp.reshape(L, B, D), 0, 1)
    ys = jnp.swapaxes(hs.reshape(Ls, Bs, D), 0, 1)
    return (yp, ys, jnp.stack(wkv_p), jnp.stack(sh_p), jnp.stack(wkv_s), jnp.stack(sh_s), jnp.stack(cv_s))
```

```python
import functools

import numpy as np
import jax
import jax.numpy as jnp
from jax import lax
from jax.experimental import pallas as pl
from jax.experimental.pallas import tpu as pltpu

F32 = jnp.float32
BF16 = jnp.bfloat16

HEAD = 64
CHUNK = 128
N_GROUPS_B = 8
EPS = 1e-6
GN_EPS = 64e-5
SQRT_HALF = float(np.sqrt(0.5))

LANES = 128
SUBLANES = 8
BF16_ROWS = 16
AUX_UNROLL = 4
N_STAGED = 5
VMEM_LIMIT = 56 * 1024 * 1024


def _const_spec(shape):
    nd = len(shape)
    return pl.BlockSpec(shape, lambda *_: (0,) * nd, pipeline_mode=pl.Buffered(1))


def _layer_spec(layer, rows, cols, col_block=0):
    return pl.BlockSpec((None, rows, cols), lambda *_: (layer, 0, col_block), pipeline_mode=pl.Buffered(1))


def _rms(x, g):
    ms = jnp.mean(x * x, axis=-1, keepdims=True)
    return x * lax.rsqrt(ms + EPS) * g


def _gelu(x):
    return 0.5 * x * (1.0 + lax.erf(x * SQRT_HALF))


def _dot(a, b):
    return jnp.dot(a, b, preferred_element_type=F32)


def _mix_a_body(h_ref, hp_ref, init_ref, gpre_ref, wrkv_ref, wga_ref, mu_rkv_ref, mu_wag_ref,
                w0_ref, w1_ref, w2_ref, a0_ref, a1_ref, a2_ref, g1_ref, g2_ref,
                r_ref, k_ref, v_ref, w_ref, a_ref, ga_ref, last_ref, *, tm, P, d):
    D = h_ref.shape[-1]
    i = pl.program_id(0)
    g = gpre_ref[...]
    xn = _rms(h_ref[...], g)
    prev = jnp.where(i == 0, init_ref[...], _rms(hp_ref[...], g))
    xn_ext = jnp.concatenate([prev, xn], axis=0)
    xs = xn_ext[P - d:P - d + tm]
    dx = xs - xn
    mw = mu_wag_ref[...]
    xw = xn + dx * mw[0:1]
    xa = xn + dx * mw[1:2]
    xg = xn + dx * mw[2:3]
    xb_ext = xn_ext.astype(BF16)
    xb = xb_ext[P:]
    mu = mu_rkv_ref[...]
    for s, out in enumerate((r_ref, k_ref, v_ref)):
        pe = _dot(xb_ext, wrkv_ref[:, s * D:(s + 1) * D])
        cur = pe[P:]
        prv = pe[P - d:P - d + tm]
        out[...] = cur + (prv - cur) * mu[:, s * D:(s + 1) * D]
    wl = w0_ref[...] + _dot(jnp.tanh(_dot(xw.astype(BF16), w1_ref[...])).astype(BF16), w2_ref[...])
    w_log = -jax.nn.softplus(-wl) - 0.5
    w_ref[...] = jnp.exp(-jnp.exp(w_log))
    al = a0_ref[...] + _dot(_dot(xa.astype(BF16), a1_ref[...]).astype(BF16), a2_ref[...])
    a_ref[...] = jax.nn.sigmoid(al)
    gg = _dot(jax.nn.sigmoid(_dot(xg.astype(BF16), g1_ref[...])).astype(BF16), g2_ref[...])
    ga = _dot(xb, wga_ref[...])
    ga_ref[...] = jax.nn.sigmoid(ga) * gg
    last_ref[...] = xn[tm - P:]


def _mix_a(h, init_xn, lw, *, tm, P, d):
    T, D = h.shape
    n_tiles = T // tm
    ratio = tm // P
    tok = pl.BlockSpec((tm, D), lambda i: (i, 0))
    in_specs = [
        tok,
        pl.BlockSpec((P, D), lambda i: (jnp.maximum(i * ratio - 1, 0), 0)),
        _const_spec((P, D)),
        _const_spec((1, D)),
        _layer_spec(lw["layer"], D, 3 * D, 0),
        _layer_spec(lw["layer"], D, D, 5),
        _const_spec((1, 3 * D)), _const_spec((3, D)),
        _const_spec((1, D)), _const_spec(lw["w1"].shape), _const_spec(lw["w2"].shape),
        _const_spec((1, D)), _const_spec(lw["a1"].shape), _const_spec(lw["a2"].shape),
        _const_spec(lw["g1"].shape), _const_spec(lw["g2"].shape),
    ]
    out_specs = [tok] * 6 + [pl.BlockSpec((P, D), lambda i: (0, 0))]
    out_shape = [jax.ShapeDtypeStruct((T, D), F32)] * 6 + [jax.ShapeDtypeStruct((P, D), F32)]
    return pl.pallas_call(
        functools.partial(_mix_a_body, tm=tm, P=P, d=d),
        grid=(n_tiles,), in_specs=in_specs, out_specs=out_specs, out_shape=out_shape,
        compiler_params=pltpu.CompilerParams(dimension_semantics=("arbitrary",), vmem_limit_bytes=VMEM_LIMIT),
        name="mix_a",
    )(h, h, init_xn, lw["mix_pre_g"], lw["w_in"], lw["w_in"], lw["mu_rkv"], lw["mu_wag"],
      lw["w0"], lw["w1"], lw["w2"], lw["a0"], lw["a1"], lw["a2"], lw["g1"], lw["g2"])


def _mix_b_tokens(h, gpre, wu, wvg, wgb, vng, vnb):
    xb = _rms(h, gpre).astype(BF16)
    u = _gelu(_dot(xb, wu))
    gv = _gelu(_dot(xb, wvg))
    mean = jnp.mean(gv, axis=-1, keepdims=True)
    cen = gv - mean
    var = jnp.mean(cen * cen, axis=-1, keepdims=True)
    vn = cen * lax.rsqrt(var + EPS) * vng + vnb
    su = jax.nn.sigmoid(_dot(xb, wgb)) * u
    return su, vn


def _mix_b_prompt_body(h_ref, gpre_ref, wu_ref, wvg_ref, wgb_ref, vng_ref, vnb_ref, ws_ref, bias_ref,
                       ob_ref, vn_scr, su_scr, cm_scr, *, tm, nb):
    s = pl.program_id(1)
    n_sub = pl.num_programs(1)
    su, vn = _mix_b_tokens(h_ref[...], gpre_ref[...], wu_ref[...], wvg_ref[...], wgb_ref[...],
                           vng_ref[...], vnb_ref[...])
    row0 = pl.multiple_of(s * tm, tm)
    for g in range(N_GROUPS_B):
        vn_scr[g, pl.ds(row0, tm), :] = vn[:, g * CHUNK:(g + 1) * CHUNK]
        su_scr[g, pl.ds(row0, tm), :] = su[:, g * CHUNK:(g + 1) * CHUNK]

    @pl.when(s == n_sub - 1)
    def _():
        row = lax.broadcasted_iota(jnp.int32, (CHUNK, CHUNK), 0)
        col = lax.broadcasted_iota(jnp.int32, (CHUNK, CHUNK), 1)
        for g in range(N_GROUPS_B):
            wm = jnp.where(row >= col, ws_ref[g], 0.0).astype(BF16)
            bias = bias_ref[:, g * CHUNK:(g + 1) * CHUNK]
            for b in range(nb):
                rows = pl.ds(b, CHUNK, stride=nb)
                mixed = _dot(wm, vn_scr[g, rows, :].astype(BF16)) + bias
                cm_scr[g, rows, :] = mixed * su_scr[g, rows, :]
        for g in range(N_GROUPS_B):
            ob_ref[:, g * CHUNK:(g + 1) * CHUNK] = cm_scr[g]


def _mix_b_prompt(h, lw, *, nb, tm):
    T, D = h.shape
    rows_per_chunk = CHUNK * nb
    n_chunks = T // rows_per_chunk
    n_sub = rows_per_chunk // tm
    wcol = lambda j: _layer_spec(lw["layer"], D, D, j)
    in_specs = [
        pl.BlockSpec((tm, D), lambda c, s: (c * n_sub + s, 0)),
        _const_spec((1, D)), wcol(3), wcol(4), wcol(6),
        _const_spec((1, D)), _const_spec((1, D)),
        _const_spec((N_GROUPS_B, CHUNK, CHUNK)), _const_spec((CHUNK, D)),
    ]
    slab = pltpu.VMEM((N_GROUPS_B, rows_per_chunk, CHUNK), F32)
    return pl.pallas_call(
        functools.partial(_mix_b_prompt_body, tm=tm, nb=nb),
        grid=(n_chunks, n_sub), in_specs=in_specs,
        out_specs=pl.BlockSpec((rows_per_chunk, D), lambda c, s: (c, 0)),
        out_shape=jax.ShapeDtypeStruct((T, D), F32),
        scratch_shapes=[slab, slab, slab],
        compiler_params=pltpu.CompilerParams(dimension_semantics=("arbitrary", "arbitrary"),
                                             vmem_limit_bytes=VMEM_LIMIT),
        name="mix_b_prompt",
    )(h, lw["mix_pre_g"], lw["w_in"], lw["w_in"], lw["w_in"], lw["vn_g"], lw["vn_b"], lw["w_s"], lw["bias_full"])


def _mix_b_sample_body(ws_ref, bs_ref, h_ref, gpre_ref, wu_ref, wvg_ref, wgb_ref, vng_ref, vnb_ref,
                       ob_ref, vn_ref, *, nb, L):
    su, vn = _mix_b_tokens(h_ref[...], gpre_ref[...], wu_ref[...], wvg_ref[...], wgb_ref[...],
                           vng_ref[...], vnb_ref[...])
    vn_ref[...] = vn
    for t in range(L):
        cols = []
        for g in range(N_GROUPS_B):
            acc = jnp.full((nb, CHUNK), bs_ref[g * L + t], F32)
            for s in range(t + 1):
                acc = acc + ws_ref[(g * L + t) * L + s] * vn[s * nb:(s + 1) * nb, g * CHUNK:(g + 1) * CHUNK]
            cols.append(acc)
        ob_ref[t * nb:(t + 1) * nb, :] = jnp.concatenate(cols, axis=1) * su[t * nb:(t + 1) * nb]


def _mix_b_sample(h, lw, *, nb, L):
    T, D = h.shape
    wcol = lambda j: _layer_spec(lw["layer"], D, D, j)
    smem = pl.BlockSpec(memory_space=pltpu.SMEM)
    in_specs = [smem, smem, _const_spec((T, D)), _const_spec((1, D)), wcol(3), wcol(4), wcol(6),
                _const_spec((1, D)), _const_spec((1, D))]
    full = pl.BlockSpec((T, D), lambda i: (0, 0))
    return pl.pallas_call(
        functools.partial(_mix_b_sample_body, nb=nb, L=L),
        grid=(1,), in_specs=in_specs, out_specs=[full, full],
        out_shape=[jax.ShapeDtypeStruct((T, D), F32)] * 2,
        compiler_params=pltpu.CompilerParams(dimension_semantics=("arbitrary",), vmem_limit_bytes=VMEM_LIMIT),
        name="mix_b_sample",
    )(lw["ws_head"], lw["bs_head"], h, lw["mix_pre_g"], lw["w_in"], lw["w_in"], lw["w_in"], lw["vn_g"], lw["vn_b"])


def _wkv_body(r_ref, k_ref, v_ref, w_ref, a_ref, s0_ref, kk_ref, ka_ref, rk_ref, lg_ref, lb_ref,
              o_ref, st_ref, vec_ref, y_ref, pend_ref, *, TT, n_heads):
    NA, BB, K2, R, V = range(N_STAGED)
    HALF = HEAD // 2

    @pl.when(pl.program_id(1) == 0)
    def _():
        st_ref[...] = s0_ref[...]

    heads = _lane_head_order(n_heads)
    n_blk = n_heads // 2
    low_half = lax.broadcasted_iota(jnp.int32, (SUBLANES, LANES), 1) < HEAD

    def to_lanes(x):
        return jnp.concatenate([x[:, h * HEAD:(h + 1) * HEAD] for h in heads], axis=0).T

    def to_lanes2(x1, x2):
        even, odd = [], []
        for blk in range(n_blk):
            a1 = x1[:, blk * LANES:(blk + 1) * LANES]
            a2 = x2[:, blk * LANES:(blk + 1) * LANES]
            even.append(jnp.where(low_half, a1, pltpu.roll(a2, HEAD, 1)))
            odd.append(jnp.where(low_half, pltpu.roll(a1, HEAD, 1), a2))
        zt = jnp.concatenate(even + odd, axis=0).T
        return zt[:HEAD], zt[HEAD:]

    def from_lanes(y):
        z = y.T
        blocks = [None] * n_heads
        for pos, h in enumerate(heads):
            blocks[h] = z[pos * SUBLANES:(pos + 1) * SUBLANES, :]
        return jnp.concatenate(blocks, axis=1)

    def stage(tt, p_prev):
        k, a = to_lanes2(k_ref[tt], a_ref[tt])
        w, r = to_lanes2(w_ref[tt], r_ref[tt])
        kk = k * kk_ref[...]
        nrm = jnp.sqrt(jnp.sum(kk * kk, axis=0, keepdims=True))
        kk = kk / jnp.maximum(nrm, 1e-12)
        p = p_prev * w
        inv_p = 1.0 / p
        vec_ref[tt, NA] = -kk * p_prev
        vec_ref[tt, BB] = kk * a * inv_p
        vec_ref[tt, K2] = k * (1.0 + (a - 1.0) * ka_ref[...]) * inv_p
        vec_ref[tt, R] = r * p
        vec_ref[tt, V] = to_lanes(v_ref[tt])
        return p

    def recur(cur, nxt, sa):
        sa_next = []
        for half in range(2):
            rows = slice(half * HALF, (half + 1) * HALF)
            sa_h = sa[rows]
            v_h = vec_ref[cur, V, rows, :]
            y_h = jnp.zeros((HALF, LANES), F32)
            san_h = jnp.zeros((HALF, LANES), F32)
            for j in range(HEAD):
                jr = slice(j, j + 1)
                sn = st_ref[j, rows, :] + sa_h * vec_ref[cur, BB, jr, :] + v_h * vec_ref[cur, K2, jr, :]
                st_ref[j, rows, :] = sn
                y_h = y_h + sn * vec_ref[cur, R, jr, :]
                san_h = san_h + sn * vec_ref[nxt, NA, jr, :]
            y_ref[cur, rows, :] = y_h
            sa_next.append(san_h)
        return jnp.concatenate(sa_next, axis=0)

    def finish(tt, slot):
        y = y_ref[slot]
        mean = jnp.mean(y, axis=0, keepdims=True)
        cen = y - mean
        var = jnp.mean(cen * cen, axis=0, keepdims=True)
        yn = cen * lax.rsqrt(var + GN_EPS) * lg_ref[...] + lb_ref[...]
        bonus = jnp.sum(vec_ref[slot, R] * vec_ref[slot, K2] * rk_ref[...], axis=0, keepdims=True)
        o_ref[tt] = from_lanes(yn + bonus * vec_ref[slot, V])

    p_end = lax.fori_loop(0, TT, stage, jnp.ones((HEAD, LANES), F32), unroll=AUX_UNROLL)
    pend_ref[...] = p_end

    sa0 = jnp.zeros((HEAD, LANES), F32)
    for j in range(HEAD):
        sa0 = sa0 + st_ref[j] * vec_ref[0, NA, j:j + 1, :]

    lax.fori_loop(0, TT, lambda tt, sa: recur(tt, jnp.minimum(tt + 1, TT - 1), sa), sa0)

    for j in range(HEAD):
        st_ref[j] = st_ref[j] * pend_ref[j:j + 1, :]

    def finish_step(tt, carry):
        finish(tt, tt)
        return carry

    lax.fori_loop(0, TT, finish_step, 0, unroll=AUX_UNROLL)


def _wkv(r, k, v, w, a, s0, lw, *, L, nb, TT):
    T, D = r.shape
    G = nb // SUBLANES
    n_heads = D // HEAD
    view = lambda x: x.reshape(L, G, SUBLANES, D)
    tok = pl.BlockSpec((TT, None, SUBLANES, D), lambda g, t: (t, g, 0, 0))
    st = pl.BlockSpec((None, HEAD, HEAD, LANES), lambda g, t: (g, 0, 0, 0))
    par = pl.BlockSpec((HEAD, LANES), lambda g, t: (0, 0))
    o, sT = pl.pallas_call(
        functools.partial(_wkv_body, TT=TT, n_heads=n_heads),
        grid=(G, L // TT),
        in_specs=[tok] * 5 + [st] + [par] * 5,
        out_specs=[tok, st],
        out_shape=[jax.ShapeDtypeStruct((L, G, SUBLANES, D), F32),
                   jax.ShapeDtypeStruct((G, HEAD, HEAD, LANES), F32)],
        scratch_shapes=[pltpu.VMEM((TT, N_STAGED, HEAD, LANES), F32), pltpu.VMEM((TT, HEAD, LANES), F32),
                        pltpu.VMEM((HEAD, LANES), F32)],
        compiler_params=pltpu.CompilerParams(dimension_semantics=("arbitrary", "arbitrary"),
                                             vmem_limit_bytes=VMEM_LIMIT),
        name="wkv",
    )(view(r), view(k), view(v), view(w), view(a), s0,
      lw["kk_l"], lw["ka_l"], lw["rk_l"], lw["lg_l"], lw["lb_l"])
    return o.reshape(T, D), sT


def _post_body(h_ref, o_ref, ga_ref, ob_ref, p_ref, gpost_ref, wout_ref, gfpre_ref, wup_ref, wdown_ref,
               gfpost_ref, wpg_ref, wpe_ref, out_ref):
    mixed = ga_ref[...] * o_ref[...] + ob_ref[...]
    h = h_ref[...] + _rms(_dot(mixed.astype(BF16), wout_ref[...]), gpost_ref[...])
    z = _dot(_rms(h, gfpre_ref[...]).astype(BF16), wup_ref[...])
    z = jnp.square(jnp.maximum(z, 0.0))
    h = h + _rms(_dot(z.astype(BF16), wdown_ref[...]), gfpost_ref[...])
    gate = jax.nn.sigmoid(_dot(h.astype(BF16), wpg_ref[...]))
    out_ref[...] = h + gate * _dot(p_ref[...].astype(BF16), wpe_ref[...])


def _post(h, o, ga, ob, p, lw, *, tm):
    T, D = h.shape
    tok = pl.BlockSpec((tm, D), lambda i: (i, 0))
    in_specs = [tok, tok, tok, tok, pl.BlockSpec((tm, p.shape[1]), lambda i: (i, 0)),
                _const_spec((1, D)), _layer_spec(lw["layer"], *lw["w_out"].shape[1:]),
                _const_spec((1, D)), _layer_spec(lw["layer"], *lw["w_up"].shape[1:]),
                _layer_spec(lw["layer"], *lw["w_down"].shape[1:]),
                _const_spec((1, D)), _layer_spec(lw["layer"], *lw["w_pg"].shape[1:]),
                _layer_spec(lw["layer"], *lw["w_pe"].shape[1:])]
    return pl.pallas_call(
        _post_body, grid=(T // tm,), in_specs=in_specs, out_specs=tok,
        out_shape=jax.ShapeDtypeStruct((T, D), F32),
        compiler_params=pltpu.CompilerParams(dimension_semantics=("arbitrary",), vmem_limit_bytes=VMEM_LIMIT),
        name="post",
    )(h, o, ga, ob, p, lw["mix_post_g"], lw["w_out"], lw["ffn_pre_g"], lw["w_up"], lw["w_down"],
      lw["ffn_post_g"], lw["w_pg"], lw["w_pe"])


def _lane_head_order(n_heads):
    return [2 * blk + par for par in range(2) for blk in range(n_heads // 2)]


def _lane_param(x, n_heads):
    xh = x.reshape(n_heads, HEAD)[np.array(_lane_head_order(n_heads))]
    return jnp.repeat(xh.T, SUBLANES, axis=1).astype(F32)


def _state_to_lanes(s):
    nb, nh = s.shape[0], s.shape[1]
    s = s.reshape(nb // SUBLANES, SUBLANES, nh, HEAD, HEAD)[:, :, np.array(_lane_head_order(nh))]
    return jnp.transpose(s, (0, 4, 3, 2, 1)).reshape(nb // SUBLANES, HEAD, HEAD, nh * SUBLANES)


def _state_from_lanes(s, nh):
    G = s.shape[0]
    s = jnp.transpose(s.reshape(G, HEAD, HEAD, nh, SUBLANES), (0, 4, 3, 2, 1))
    s = s[:, :, np.argsort(np.array(_lane_head_order(nh)))]
    return s.reshape(G * SUBLANES, nh, HEAD, HEAD)


def _layer(h, p, s0, init_xn, lw, *, L, nb, prompt):
    D = h.shape[1]
    nh = D // HEAD
    if prompt:
        tm, P = 256, BF16_ROWS
        r, k, v, w, a, ga, last = _mix_a(h, init_xn, lw, tm=tm, P=P, d=nb)
        ob = _mix_b_prompt(h, lw, nb=nb, tm=256)
        vn = None
        TT = 32
    else:
        r, k, v, w, a, ga, last = _mix_a(h, init_xn, lw, tm=nb, P=nb, d=nb)
        ob, vn = _mix_b_sample(h, lw, nb=nb, L=L)
        TT = L
    o, sT = _wkv(r, k, v, w, a, s0, lw, L=L, nb=nb, TT=TT)
    h = _post(h, o, ga, ob, p, lw, tm=min(256, h.shape[0]))
    return h, _state_from_lanes(sT, nh), last[-nb:], vn


def kernel(x_prompt, x_sample, state_wkv, state_shift, p_prompt, p_sample, mix_pre_g, mix_post_g, ffn_pre_g, ffn_post_g, w_in, mu_rkv, mu_wag, w0, w1, w2, a0, a1, a2, g1, g2, k_k, k_a, r_k, lnx_g, lnx_b, vn_g, vn_b, w_s, b_s, w_out, w_up, w_down, w_pe, w_pg):
    B, L, D = x_prompt.shape
    Bs, Ls, _ = x_sample.shape
    depth = w_in.shape[0]
    nh = D // HEAD
    row = lambda x: x.reshape(1, -1).astype(F32)
    tmajor = lambda x: jnp.swapaxes(x, 0, 1).reshape(-1, x.shape[-1])

    hp = tmajor(x_prompt)
    hs = tmajor(x_sample)
    zero_state = jnp.zeros((B // SUBLANES, HEAD, HEAD, nh * SUBLANES), F32)
    big = dict(w_in=w_in.astype(BF16), w_out=w_out.astype(BF16), w_up=w_up.astype(BF16),
               w_down=w_down.astype(BF16), w_pe=w_pe.astype(BF16), w_pg=w_pg.astype(BF16))
    wkv_p, sh_p, wkv_s, sh_s, cv_s = [], [], [], [], []
    for i in range(depth):
        lw = dict(
            big, layer=i,
            mix_pre_g=row(mix_pre_g[i]), mix_post_g=row(mix_post_g[i]),
            ffn_pre_g=row(ffn_pre_g[i]), ffn_post_g=row(ffn_post_g[i]),
            mu_rkv=row(mu_rkv[i]), mu_wag=mu_wag[i],
            w0=row(w0[i]), w1=w1[i].astype(BF16), w2=w2[i].astype(BF16),
            a0=row(a0[i]), a1=a1[i].astype(BF16), a2=a2[i].astype(BF16),
            g1=g1[i].astype(BF16), g2=g2[i].astype(BF16),
            kk_l=_lane_param(k_k[i], nh), ka_l=_lane_param(k_a[i], nh), rk_l=_lane_param(r_k[i].reshape(-1), nh),
            lg_l=_lane_param(lnx_g[i], nh), lb_l=_lane_param(lnx_b[i], nh),
            vn_g=row(vn_g[i]), vn_b=row(vn_b[i]), w_s=w_s[i],
            bias_full=jnp.repeat(b_s[i].T, CHUNK, axis=1),
            ws_head=w_s[i][:, :Ls, :Ls].reshape(-1), bs_head=b_s[i][:, :Ls].reshape(-1),
        )
        hp, S_p, last_p, _ = _layer(hp, tmajor(p_prompt[i]), zero_state, jnp.zeros((BF16_ROWS, D), F32), lw,
                                    L=L, nb=B, prompt=True)
        hs, S_s, last_s, vn_s = _layer(hs, tmajor(p_sample[i]), _state_to_lanes(state_wkv[i]), state_shift[i], lw,
                                       L=Ls, nb=Bs, prompt=False)
        wkv_p.append(S_p)
        sh_p.append(last_p)
        wkv_s.append(S_s)
        sh_s.append(last_s)
        cv_s.append(jnp.swapaxes(vn_s.reshape(Ls, Bs, D), 0, 1))
    yp = jnp.swapaxes(hp.reshape(L, B, D), 0, 1)
    ys = jnp.swapaxes(hs.reshape(Ls, Bs, D), 0, 1)
    return (yp, ys, jnp.stack(wkv_p), jnp.stack(sh_p), jnp.stack(wkv_s), jnp.stack(sh_s), jnp.stack(cv_s))
```

```python
import functools

import numpy as np
import jax
import jax.numpy as jnp
from jax import lax
from jax.experimental import pallas as pl
from jax.experimental.pallas import tpu as pltpu

F32 = jnp.float32
BF16 = jnp.bfloat16

HEAD = 64
CHUNK = 128
N_GROUPS_B = 8
EPS = 1e-6
GN_EPS = 64e-5
SQRT_HALF = float(np.sqrt(0.5))
EXP_NEG_HALF = float(np.exp(-0.5))

LANES = 128
SUBLANES = 8
BF16_ROWS = 16
AUX_UNROLL = 4
N_STAGED = 5
WKV_BLOCK = 32
VMEM_LIMIT = 56 * 1024 * 1024


def _const_spec(shape):
    nd = len(shape)
    return pl.BlockSpec(shape, lambda *_: (0,) * nd, pipeline_mode=pl.Buffered(1))


def _layer_spec(layer, rows, cols, col_block=0):
    return pl.BlockSpec((None, rows, cols), lambda *_: (layer, 0, col_block), pipeline_mode=pl.Buffered(1))


def _rms(x, g):
    ms = jnp.mean(x * x, axis=-1, keepdims=True)
    return x * lax.rsqrt(ms + EPS) * g


def _gelu(x):
    return 0.5 * x * (1.0 + lax.erf(x * SQRT_HALF))


def _dot(a, b):
    return jnp.dot(a, b, preferred_element_type=F32)


def _mix_a_body(h_ref, hp_ref, init_ref, gpre_ref, wrkv_ref, wga_ref, mu_rkv_ref, mu_wag_ref,
                w0_ref, w1_ref, w2_ref, a0_ref, a1_ref, a2_ref, g1_ref, g2_ref,
                r_ref, k_ref, v_ref, w_ref, a_ref, ga_ref, last_ref, *, tm, P, d):
    D = h_ref.shape[-1]
    i = pl.program_id(0)
    g = gpre_ref[...]
    xn = _rms(h_ref[...], g)
    prev = jnp.where(i == 0, init_ref[...], _rms(hp_ref[...], g))
    xn_ext = jnp.concatenate([prev, xn], axis=0)
    xs = xn_ext[P - d:P - d + tm]
    dx = xs - xn
    mw = mu_wag_ref[...]
    xw = xn + dx * mw[0:1]
    xa = xn + dx * mw[1:2]
    xg = xn + dx * mw[2:3]
    xb_ext = xn_ext.astype(BF16)
    xb = xb_ext[P:]
    mu = mu_rkv_ref[...]

    def shifted_proj(s, out):
        pe = _dot(xb_ext, wrkv_ref[:, s * D:(s + 1) * D])
        cur = pe[P:]
        prv = pe[P - d:P - d + tm]
        out[...] = cur + (prv - cur) * mu[:, s * D:(s + 1) * D]

    lw1 = jnp.tanh(_dot(xw.astype(BF16), w1_ref[...])).astype(BF16)
    la1 = _dot(xa.astype(BF16), a1_ref[...]).astype(BF16)
    lg1 = jax.nn.sigmoid(_dot(xg.astype(BF16), g1_ref[...])).astype(BF16)
    shifted_proj(0, r_ref)
    wl = w0_ref[...] + _dot(lw1, w2_ref[...])
    w_ref[...] = jnp.exp(-EXP_NEG_HALF * jax.nn.sigmoid(wl))
    shifted_proj(1, k_ref)
    a_ref[...] = jax.nn.sigmoid(a0_ref[...] + _dot(la1, a2_ref[...]))
    shifted_proj(2, v_ref)
    ga_ref[...] = jax.nn.sigmoid(_dot(xb, wga_ref[...])) * _dot(lg1, g2_ref[...])
    last_ref[...] = xn[tm - P:]


def _mix_a(h, init_xn, lw, *, tm, P, d):
    T, D = h.shape
    n_tiles = T // tm
    ratio = tm // P
    tok = pl.BlockSpec((tm, D), lambda i: (i, 0))
    in_specs = [
        tok,
        pl.BlockSpec((P, D), lambda i: (jnp.maximum(i * ratio - 1, 0), 0)),
        _const_spec((P, D)),
        _const_spec((1, D)),
        _layer_spec(lw["layer"], D, 3 * D, 0),
        _layer_spec(lw["layer"], D, D, 5),
        _const_spec((1, 3 * D)), _const_spec((3, D)),
        _const_spec((1, D)), _const_spec(lw["w1"].shape), _const_spec(lw["w2"].shape),
        _const_spec((1, D)), _const_spec(lw["a1"].shape), _const_spec(lw["a2"].shape),
        _const_spec(lw["g1"].shape), _const_spec(lw["g2"].shape),
    ]
    out_specs = [tok] * 6 + [pl.BlockSpec((P, D), lambda i: (0, 0))]
    out_shape = [jax.ShapeDtypeStruct((T, D), F32)] * 6 + [jax.ShapeDtypeStruct((P, D), F32)]
    return pl.pallas_call(
        functools.partial(_mix_a_body, tm=tm, P=P, d=d),
        grid=(n_tiles,), in_specs=in_specs, out_specs=out_specs, out_shape=out_shape,
        compiler_params=pltpu.CompilerParams(dimension_semantics=("arbitrary",), vmem_limit_bytes=VMEM_LIMIT),
        name="mix_a",
    )(h, h, init_xn, lw["mix_pre_g"], lw["w_in"], lw["w_in"], lw["mu_rkv"], lw["mu_wag"],
      lw["w0"], lw["w1"], lw["w2"], lw["a0"], lw["a1"], lw["a2"], lw["g1"], lw["g2"])


def _mix_b_tokens(h, gpre, wu, wvg, wgb, vng, vnb):
    xb = _rms(h, gpre).astype(BF16)
    gv = _gelu(_dot(xb, wvg))
    mean = jnp.mean(gv, axis=-1, keepdims=True)
    cen = gv - mean
    var = jnp.mean(cen * cen, axis=-1, keepdims=True)
    u = _gelu(_dot(xb, wu))
    vn = cen * lax.rsqrt(var + EPS) * vng + vnb
    su = jax.nn.sigmoid(_dot(xb, wgb)) * u
    return su, vn


def _mix_b_prompt_body(h_ref, gpre_ref, wu_ref, wvg_ref, wgb_ref, vng_ref, vnb_ref, ws_ref, bias_ref,
                       ob_ref, vn_scr, su_scr, cm_scr, *, tm, nb):
    s = pl.program_id(1)
    n_sub = pl.num_programs(1)
    su, vn = _mix_b_tokens(h_ref[...], gpre_ref[...], wu_ref[...], wvg_ref[...], wgb_ref[...],
                           vng_ref[...], vnb_ref[...])
    row0 = pl.multiple_of(s * tm, tm)
    for g in range(N_GROUPS_B):
        vn_scr[g, pl.ds(row0, tm), :] = vn[:, g * CHUNK:(g + 1) * CHUNK]
        su_scr[g, pl.ds(row0, tm), :] = su[:, g * CHUNK:(g + 1) * CHUNK]

    @pl.when(s == n_sub - 1)
    def _():
        row = lax.broadcasted_iota(jnp.int32, (CHUNK, CHUNK), 0)
        col = lax.broadcasted_iota(jnp.int32, (CHUNK, CHUNK), 1)
        for g in range(N_GROUPS_B):
            wm = jnp.where(row >= col, ws_ref[g], 0.0).astype(BF16)
            bias = bias_ref[:, g * CHUNK:(g + 1) * CHUNK]
            for b in range(nb):
                rows = pl.ds(b, CHUNK, stride=nb)
                mixed = _dot(wm, vn_scr[g, rows, :].astype(BF16)) + bias
                cm_scr[g, rows, :] = mixed * su_scr[g, rows, :]
        for g in range(N_GROUPS_B):
            ob_ref[:, g * CHUNK:(g + 1) * CHUNK] = cm_scr[g]


def _mix_b_prompt(h, lw, *, nb, tm):
    T, D = h.shape
    rows_per_chunk = CHUNK * nb
    n_chunks = T // rows_per_chunk
    n_sub = rows_per_chunk // tm
    wcol = lambda j: _layer_spec(lw["layer"], D, D, j)
    in_specs = [
        pl.BlockSpec((tm, D), lambda c, s: (c * n_sub + s, 0)),
        _const_spec((1, D)), wcol(3), wcol(4), wcol(6),
        _const_spec((1, D)), _const_spec((1, D)),
        _const_spec((N_GROUPS_B, CHUNK, CHUNK)), _const_spec((CHUNK, D)),
    ]
    slab = pltpu.VMEM((N_GROUPS_B, rows_per_chunk, CHUNK), F32)
    return pl.pallas_call(
        functools.partial(_mix_b_prompt_body, tm=tm, nb=nb),
        grid=(n_chunks, n_sub), in_specs=in_specs,
        out_specs=pl.BlockSpec((rows_per_chunk, D), lambda c, s: (c, 0)),
        out_shape=jax.ShapeDtypeStruct((T, D), F32),
        scratch_shapes=[slab, slab, slab],
        compiler_params=pltpu.CompilerParams(dimension_semantics=("arbitrary", "arbitrary"),
                                             vmem_limit_bytes=VMEM_LIMIT),
        name="mix_b_prompt",
    )(h, lw["mix_pre_g"], lw["w_in"], lw["w_in"], lw["w_in"], lw["vn_g"], lw["vn_b"], lw["w_s"], lw["bias_full"])


def _mix_b_sample_body(ws_ref, bs_ref, h_ref, gpre_ref, wu_ref, wvg_ref, wgb_ref, vng_ref, vnb_ref,
                       ob_ref, vn_ref, *, nb, L):
    su, vn = _mix_b_tokens(h_ref[...], gpre_ref[...], wu_ref[...], wvg_ref[...], wgb_ref[...],
                           vng_ref[...], vnb_ref[...])
    vn_ref[...] = vn
    for t in range(L):
        cols = []
        for g in range(N_GROUPS_B):
            acc = jnp.full((nb, CHUNK), bs_ref[g * L + t], F32)
            for s in range(t + 1):
                acc = acc + ws_ref[(g * L + t) * L + s] * vn[s * nb:(s + 1) * nb, g * CHUNK:(g + 1) * CHUNK]
            cols.append(acc)
        ob_ref[t * nb:(t + 1) * nb, :] = jnp.concatenate(cols, axis=1) * su[t * nb:(t + 1) * nb]


def _mix_b_sample(h, lw, *, nb, L):
    T, D = h.shape
    wcol = lambda j: _layer_spec(lw["layer"], D, D, j)
    smem = pl.BlockSpec(memory_space=pltpu.SMEM)
    in_specs = [smem, smem, _const_spec((T, D)), _const_spec((1, D)), wcol(3), wcol(4), wcol(6),
                _const_spec((1, D)), _const_spec((1, D))]
    full = pl.BlockSpec((T, D), lambda i: (0, 0))
    return pl.pallas_call(
        functools.partial(_mix_b_sample_body, nb=nb, L=L),
        grid=(1,), in_specs=in_specs, out_specs=[full, full],
        out_shape=[jax.ShapeDtypeStruct((T, D), F32)] * 2,
        compiler_params=pltpu.CompilerParams(dimension_semantics=("arbitrary",), vmem_limit_bytes=VMEM_LIMIT),
        name="mix_b_sample",
    )(lw["ws_head"], lw["bs_head"], h, lw["mix_pre_g"], lw["w_in"], lw["w_in"], lw["w_in"], lw["vn_g"], lw["vn_b"])


def _wkv_body(r_ref, k_ref, v_ref, w_ref, a_ref, s0_ref, kk_ref, ka_ref, rk_ref, lg_ref, lb_ref,
              o_ref, st_ref, vec_ref, y_ref, pend_ref, *, TT, n_heads):
    NA, BB, K2, R, V = range(N_STAGED)
    HALF = HEAD // 2

    @pl.when(pl.program_id(1) == 0)
    def _():
        st_ref[...] = s0_ref[...]

    heads = _lane_head_order(n_heads)
    n_blk = n_heads // 2
    low_half = lax.broadcasted_iota(jnp.int32, (SUBLANES, LANES), 1) < HEAD

    def to_lanes(x):
        return jnp.concatenate([x[:, h * HEAD:(h + 1) * HEAD] for h in heads], axis=0).T

    def to_lanes2(x1, x2):
        even, odd = [], []
        for blk in range(n_blk):
            a1 = x1[:, blk * LANES:(blk + 1) * LANES]
            a2 = x2[:, blk * LANES:(blk + 1) * LANES]
            even.append(jnp.where(low_half, a1, pltpu.roll(a2, HEAD, 1)))
            odd.append(jnp.where(low_half, pltpu.roll(a1, HEAD, 1), a2))
        zt = jnp.concatenate(even + odd, axis=0).T
        return zt[:HEAD], zt[HEAD:]

    def from_lanes(y):
        z = y.T
        blocks = [None] * n_heads
        for pos, h in enumerate(heads):
            blocks[h] = z[pos * SUBLANES:(pos + 1) * SUBLANES, :]
        return jnp.concatenate(blocks, axis=1)

    def stage(tt, p_prev):
        k, a = to_lanes2(k_ref[tt], a_ref[tt])
        w, r = to_lanes2(w_ref[tt], r_ref[tt])
        kk = k * kk_ref[...]
        nrm = jnp.sqrt(jnp.sum(kk * kk, axis=0, keepdims=True))
        kk = kk / jnp.maximum(nrm, 1e-12)
        p = p_prev * w
        inv_p = 1.0 / p
        vec_ref[tt, NA] = -kk * p_prev
        vec_ref[tt, BB] = kk * a * inv_p
        vec_ref[tt, K2] = k * (1.0 + (a - 1.0) * ka_ref[...]) * inv_p
        vec_ref[tt, R] = r * p
        vec_ref[tt, V] = to_lanes(v_ref[tt])
        return p

    def recur(cur, nxt, sa):
        sa_next = []
        for half in range(2):
            rows = slice(half * HALF, (half + 1) * HALF)
            sa_h = sa[rows]
            v_h = vec_ref[cur, V, rows, :]
            y_h = jnp.zeros((HALF, LANES), F32)
            san_h = jnp.zeros((HALF, LANES), F32)
            for j in range(HEAD):
                jr = slice(j, j + 1)
                sn = st_ref[j, rows, :] + sa_h * vec_ref[cur, BB, jr, :] + v_h * vec_ref[cur, K2, jr, :]
                st_ref[j, rows, :] = sn
                y_h = y_h + sn * vec_ref[cur, R, jr, :]
                san_h = san_h + sn * vec_ref[nxt, NA, jr, :]
            y_ref[cur, rows, :] = y_h
            sa_next.append(san_h)
        return jnp.concatenate(sa_next, axis=0)

    def finish(tt, slot):
        y = y_ref[slot]
        mean = jnp.mean(y, axis=0, keepdims=True)
        cen = y - mean
        var = jnp.mean(cen * cen, axis=0, keepdims=True)
        yn = cen * lax.rsqrt(var + GN_EPS) * lg_ref[...] + lb_ref[...]
        bonus = jnp.sum(vec_ref[slot, R] * vec_ref[slot, K2] * rk_ref[...], axis=0, keepdims=True)
        o_ref[tt] = from_lanes(yn + bonus * vec_ref[slot, V])

    p_end = lax.fori_loop(0, TT, stage, jnp.ones((HEAD, LANES), F32), unroll=min(AUX_UNROLL, TT))
    pend_ref[...] = p_end

    sa0 = jnp.zeros((HEAD, LANES), F32)
    for j in range(HEAD):
        sa0 = sa0 + st_ref[j] * vec_ref[0, NA, j:j + 1, :]

    lax.fori_loop(0, TT, lambda tt, sa: recur(tt, jnp.minimum(tt + 1, TT - 1), sa), sa0)

    for j in range(HEAD):
        st_ref[j] = st_ref[j] * pend_ref[j:j + 1, :]

    def finish_step(tt, carry):
        finish(tt, tt)
        return carry

    lax.fori_loop(0, TT, finish_step, 0, unroll=min(AUX_UNROLL, TT))


def _wkv(r, k, v, w, a, s0, lw, *, L, nb, TT):
    T, D = r.shape
    G = nb // SUBLANES
    n_heads = D // HEAD
    view = lambda x: x.reshape(L, G, SUBLANES, D)
    tok = pl.BlockSpec((TT, None, SUBLANES, D), lambda g, t: (t, g, 0, 0))
    st = pl.BlockSpec((HEAD, HEAD, LANES), lambda g, t: (0, 0, g))
    par = pl.BlockSpec((HEAD, LANES), lambda g, t: (0, 0))
    o, sT = pl.pallas_call(
        functools.partial(_wkv_body, TT=TT, n_heads=n_heads),
        grid=(G, L // TT),
        in_specs=[tok] * 5 + [st] + [par] * 5,
        out_specs=[tok, st],
        out_shape=[jax.ShapeDtypeStruct((L, G, SUBLANES, D), F32),
                   jax.ShapeDtypeStruct((HEAD, HEAD, G * LANES), F32)],
        scratch_shapes=[pltpu.VMEM((TT, N_STAGED, HEAD, LANES), F32), pltpu.VMEM((TT, HEAD, LANES), F32),
                        pltpu.VMEM((HEAD, LANES), F32)],
        compiler_params=pltpu.CompilerParams(dimension_semantics=("arbitrary", "arbitrary"),
                                             vmem_limit_bytes=VMEM_LIMIT),
        name="wkv",
    )(view(r), view(k), view(v), view(w), view(a), s0,
      lw["kk_l"], lw["ka_l"], lw["rk_l"], lw["lg_l"], lw["lb_l"])
    return o.reshape(T, D), sT


def _post_body(h_ref, o_ref, ga_ref, ob_ref, p_ref, gpost_ref, wout_ref, gfpre_ref, wup_ref, wdown_ref,
               gfpost_ref, wpg_ref, wpe_ref, out_ref):
    mixed = ga_ref[...] * o_ref[...] + ob_ref[...]
    h = h_ref[...] + _rms(_dot(mixed.astype(BF16), wout_ref[...]), gpost_ref[...])
    z = _dot(_rms(h, gfpre_ref[...]).astype(BF16), wup_ref[...])
    z = jnp.square(jnp.maximum(z, 0.0))
    h = h + _rms(_dot(z.astype(BF16), wdown_ref[...]), gfpost_ref[...])
    gate = jax.nn.sigmoid(_dot(h.astype(BF16), wpg_ref[...]))
    out_ref[...] = h + gate * _dot(p_ref[...].astype(BF16), wpe_ref[...])


def _post(h, o, ga, ob, p, lw, *, tm):
    T, D = h.shape
    tok = pl.BlockSpec((tm, D), lambda i: (i, 0))
    in_specs = [tok, tok, tok, tok, pl.BlockSpec((tm, p.shape[1]), lambda i: (i, 0)),
                _const_spec((1, D)), _layer_spec(lw["layer"], *lw["w_out"].shape[1:]),
                _const_spec((1, D)), _layer_spec(lw["layer"], *lw["w_up"].shape[1:]),
                _layer_spec(lw["layer"], *lw["w_down"].shape[1:]),
                _const_spec((1, D)), _layer_spec(lw["layer"], *lw["w_pg"].shape[1:]),
                _layer_spec(lw["layer"], *lw["w_pe"].shape[1:])]
    return pl.pallas_call(
        _post_body, grid=(T // tm,), in_specs=in_specs, out_specs=tok,
        out_shape=jax.ShapeDtypeStruct((T, D), F32),
        compiler_params=pltpu.CompilerParams(dimension_semantics=("arbitrary",), vmem_limit_bytes=VMEM_LIMIT),
        name="post",
    )(h, o, ga, ob, p, lw["mix_post_g"], lw["w_out"], lw["ffn_pre_g"], lw["w_up"], lw["w_down"],
      lw["ffn_post_g"], lw["w_pg"], lw["w_pe"])


def _lane_head_order(n_heads):
    return [2 * blk + par for par in range(2) for blk in range(n_heads // 2)]


def _lane_param(x, n_heads):
    xh = jnp.swapaxes(x.reshape(n_heads // 2, 2, HEAD), 0, 1).reshape(n_heads, HEAD)
    return jnp.repeat(xh.T, SUBLANES, axis=1).astype(F32)


def _state_to_lanes(s):
    nb, nh = s.shape[0], s.shape[1]
    s = s.reshape(nb // SUBLANES, SUBLANES, nh // 2, 2, HEAD, HEAD)
    s = jnp.transpose(s, (0, 3, 2, 1, 5, 4)).reshape(nb * nh, HEAD * HEAD)
    return s.T.reshape(HEAD, HEAD, nb * nh)


def _state_from_lanes(s, nh):
    n_lanes = s.shape[-1]
    G = n_lanes // (nh * SUBLANES)
    s = s.reshape(HEAD * HEAD, n_lanes).T.reshape(G, 2, nh // 2, SUBLANES, HEAD, HEAD)
    return jnp.transpose(s, (0, 3, 2, 1, 5, 4)).reshape(G * SUBLANES, nh, HEAD, HEAD)


def _layer(h, p, s0, init_xn, lw, *, L, nb, prompt):
    D = h.shape[1]
    nh = D // HEAD
    if prompt:
        tm, P = 256, BF16_ROWS
        r, k, v, w, a, ga, last = _mix_a(h, init_xn, lw, tm=tm, P=P, d=nb)
        ob = _mix_b_prompt(h, lw, nb=nb, tm=256)
        vn = None
        TT = WKV_BLOCK
    else:
        r, k, v, w, a, ga, last = _mix_a(h, init_xn, lw, tm=nb, P=nb, d=nb)
        ob, vn = _mix_b_sample(h, lw, nb=nb, L=L)
        TT = L
    o, sT = _wkv(r, k, v, w, a, s0, lw, L=L, nb=nb, TT=TT)
    h = _post(h, o, ga, ob, p, lw, tm=min(256, h.shape[0]))
    return h, _state_from_lanes(sT, nh), last[-nb:], vn


def kernel(x_prompt, x_sample, state_wkv, state_shift, p_prompt, p_sample, mix_pre_g, mix_post_g, ffn_pre_g, ffn_post_g, w_in, mu_rkv, mu_wag, w0, w1, w2, a0, a1, a2, g1, g2, k_k, k_a, r_k, lnx_g, lnx_b, vn_g, vn_b, w_s, b_s, w_out, w_up, w_down, w_pe, w_pg):
    B, L, D = x_prompt.shape
    Bs, Ls, _ = x_sample.shape
    depth = w_in.shape[0]
    nh = D // HEAD
    row = lambda x: x.reshape(1, -1).astype(F32)
    tmajor = lambda x: jnp.swapaxes(x, 0, 1).reshape(-1, x.shape[-1])

    hp = tmajor(x_prompt)
    hs = tmajor(x_sample)
    zero_state = jnp.zeros((HEAD, HEAD, B * nh), F32)
    big = dict(w_in=w_in.astype(BF16), w_out=w_out.astype(BF16), w_up=w_up.astype(BF16),
               w_down=w_down.astype(BF16), w_pe=w_pe.astype(BF16), w_pg=w_pg.astype(BF16))
    wkv_p, sh_p, wkv_s, sh_s, cv_s = [], [], [], [], []
    for i in range(depth):
        lw = dict(
            big, layer=i,
            mix_pre_g=row(mix_pre_g[i]), mix_post_g=row(mix_post_g[i]),
            ffn_pre_g=row(ffn_pre_g[i]), ffn_post_g=row(ffn_post_g[i]),
            mu_rkv=row(mu_rkv[i]), mu_wag=mu_wag[i],
            w0=row(w0[i]), w1=w1[i].astype(BF16), w2=w2[i].astype(BF16),
            a0=row(a0[i]), a1=a1[i].astype(BF16), a2=a2[i].astype(BF16),
            g1=g1[i].astype(BF16), g2=g2[i].astype(BF16),
            kk_l=_lane_param(k_k[i], nh), ka_l=_lane_param(k_a[i], nh), rk_l=_lane_param(r_k[i].reshape(-1), nh),
            lg_l=_lane_param(lnx_g[i], nh), lb_l=_lane_param(lnx_b[i], nh),
            vn_g=row(vn_g[i]), vn_b=row(vn_b[i]), w_s=w_s[i],
            bias_full=jnp.repeat(b_s[i].T, CHUNK, axis=1),
            ws_head=w_s[i][:, :Ls, :Ls].reshape(-1), bs_head=b_s[i][:, :Ls].reshape(-1),
        )
        hp, S_p, last_p, _ = _layer(hp, tmajor(p_prompt[i]), zero_state, jnp.zeros((BF16_ROWS, D), F32), lw,
                                    L=L, nb=B, prompt=True)
        hs, S_s, last_s, vn_s = _layer(hs, tmajor(p_sample[i]), _state_to_lanes(state_wkv[i]), state_shift[i], lw,
                                       L=Ls, nb=Bs, prompt=False)
        wkv_p.append(S_p)
        sh_p.append(last_p)
        wkv_s.append(S_s)
        sh_s.append(last_s)
        cv_s.append(jnp.swapaxes(vn_s.reshape(Ls, Bs, D), 0, 1))
    yp = jnp.swapaxes(hp.reshape(L, B, D), 0, 1)
    ys = jnp.swapaxes(hs.reshape(Ls, Bs, D), 0, 1)
    return (yp, ys, jnp.stack(wkv_p), jnp.stack(sh_p), jnp.stack(wkv_s), jnp.stack(sh_s), jnp.stack(cv_s))
```

```python
import functools

import numpy as np
import jax
import jax.numpy as jnp
from jax import lax
from jax.experimental import pallas as pl
from jax.experimental.pallas import tpu as pltpu

F32 = jnp.float32
BF16 = jnp.bfloat16

HEAD = 64
CHUNK = 128
N_GROUPS_B = 8
EPS = 1e-6
GN_EPS = 64e-5
SQRT_HALF = float(np.sqrt(0.5))
EXP_NEG_HALF = float(np.exp(-0.5))

LANES = 128
SUBLANES = 8
BF16_ROWS = 16
AUX_UNROLL = 4
N_STAGED = 4
WKV_BLOCK = 32
VMEM_LIMIT = 56 * 1024 * 1024


def _const_spec(shape):
    nd = len(shape)
    return pl.BlockSpec(shape, lambda *_: (0,) * nd, pipeline_mode=pl.Buffered(1))


def _layer_spec(layer, rows, cols, col_block=0):
    return pl.BlockSpec((None, rows, cols), lambda *_: (layer, 0, col_block), pipeline_mode=pl.Buffered(1))


def _rms(x, g):
    ms = jnp.mean(x * x, axis=-1, keepdims=True)
    return x * lax.rsqrt(ms + EPS) * g


def _gelu(x):
    return 0.5 * x * (1.0 + lax.erf(x * SQRT_HALF))


def _dot(a, b):
    return jnp.dot(a, b, preferred_element_type=F32)


def _lane_head_order(n_heads):
    return [2 * blk + par for par in range(2) for blk in range(n_heads // 2)]


def _to_lanes(x):
    heads = _lane_head_order(x.shape[1] // HEAD)
    return jnp.concatenate([x[:, h * HEAD:(h + 1) * HEAD] for h in heads], axis=0).T


def _to_lanes2(x1, x2):
    low_half = lax.broadcasted_iota(jnp.int32, (SUBLANES, LANES), 1) < HEAD
    even, odd = [], []
    for blk in range(x1.shape[1] // LANES):
        a1 = x1[:, blk * LANES:(blk + 1) * LANES]
        a2 = x2[:, blk * LANES:(blk + 1) * LANES]
        even.append(jnp.where(low_half, a1, pltpu.roll(a2, HEAD, 1)))
        odd.append(jnp.where(low_half, pltpu.roll(a1, HEAD, 1), a2))
    zt = jnp.concatenate(even + odd, axis=0).T
    return zt[:HEAD], zt[HEAD:]


def _from_lanes(y):
    n_heads = LANES // SUBLANES
    z = y.T
    blocks = [None] * n_heads
    for pos, h in enumerate(_lane_head_order(n_heads)):
        blocks[h] = z[pos * SUBLANES:(pos + 1) * SUBLANES, :]
    return jnp.concatenate(blocks, axis=1)


def _slabs(tm, nb):
    tg = nb // SUBLANES
    return [(s, g, (s * tg + g) * SUBLANES) for s in range(tm // nb) for g in range(tg)]


def _mix_a_body(h_ref, hp_ref, init_ref, gpre_ref, wrkv_ref, wga_ref, mu_rkv_ref, mu_wag_ref,
                w0_ref, w1_ref, w2_ref, a0_ref, a1_ref, a2_ref, g1_ref, g2_ref,
                r_ref, k_ref, v_ref, w_ref, a_ref, ga_ref, last_ref, *, tm, P, d):
    D = h_ref.shape[-1]
    slabs = _slabs(tm, d)
    i = pl.program_id(0)
    g = gpre_ref[...]
    xn = _rms(h_ref[...], g)
    prev = jnp.where(i == 0, init_ref[...], _rms(hp_ref[...], g))
    xn_ext = jnp.concatenate([prev, xn], axis=0)
    xs = xn_ext[P - d:P - d + tm]
    dx = xs - xn
    mw = mu_wag_ref[...]
    xw = xn + dx * mw[0:1]
    xa = xn + dx * mw[1:2]
    xg = xn + dx * mw[2:3]
    xb_ext = xn_ext.astype(BF16)
    xb = xb_ext[P:]
    mu = mu_rkv_ref[...]

    def shifted_proj(s):
        pe = _dot(xb_ext, wrkv_ref[:, s * D:(s + 1) * D])
        cur = pe[P:]
        prv = pe[P - d:P - d + tm]
        return cur + (prv - cur) * mu[:, s * D:(s + 1) * D]

    def put_pair(x1, x2, out1, out2):
        for s, g, r0 in slabs:
            out1[s, g], out2[s, g] = _to_lanes2(x1[r0:r0 + SUBLANES], x2[r0:r0 + SUBLANES])

    lw1 = jnp.tanh(_dot(xw.astype(BF16), w1_ref[...])).astype(BF16)
    la1 = _dot(xa.astype(BF16), a1_ref[...]).astype(BF16)
    lg1 = jax.nn.sigmoid(_dot(xg.astype(BF16), g1_ref[...])).astype(BF16)
    r = shifted_proj(0)
    wl = w0_ref[...] + _dot(lw1, w2_ref[...])
    w = jnp.exp(-EXP_NEG_HALF * jax.nn.sigmoid(wl))
    k = shifted_proj(1)
    put_pair(w, r, w_ref, r_ref)
    a = jax.nn.sigmoid(a0_ref[...] + _dot(la1, a2_ref[...]))
    v = shifted_proj(2)
    put_pair(k, a, k_ref, a_ref)
    ga_ref[...] = jax.nn.sigmoid(_dot(xb, wga_ref[...])) * _dot(lg1, g2_ref[...])
    for s, g, r0 in slabs:
        v_ref[s, g] = _to_lanes(v[r0:r0 + SUBLANES])
    last_ref[...] = xn[tm - P:]


def _mix_a(h, init_xn, lw, *, tm, P, d):
    T, D = h.shape
    n_tiles = T // tm
    ratio = tm // P
    tok = pl.BlockSpec((tm, D), lambda i: (i, 0))
    in_specs = [
        tok,
        pl.BlockSpec((P, D), lambda i: (jnp.maximum(i * ratio - 1, 0), 0)),
        _const_spec((P, D)),
        _const_spec((1, D)),
        _layer_spec(lw["layer"], D, 3 * D, 0),
        _layer_spec(lw["layer"], D, D, 5),
        _const_spec((1, 3 * D)), _const_spec((3, D)),
        _const_spec((1, D)), _const_spec(lw["w1"].shape), _const_spec(lw["w2"].shape),
        _const_spec((1, D)), _const_spec(lw["a1"].shape), _const_spec(lw["a2"].shape),
        _const_spec(lw["g1"].shape), _const_spec(lw["g2"].shape),
    ]
    ts, tg = tm // d, d // SUBLANES
    lane = pl.BlockSpec((ts, tg, HEAD, LANES), lambda i: (i, 0, 0, 0))
    lane_shape = jax.ShapeDtypeStruct((T // d, tg, HEAD, LANES), F32)
    out_specs = [lane] * 5 + [tok, pl.BlockSpec((P, D), lambda i: (0, 0))]
    out_shape = [lane_shape] * 5 + [jax.ShapeDtypeStruct((T, D), F32), jax.ShapeDtypeStruct((P, D), F32)]
    return pl.pallas_call(
        functools.partial(_mix_a_body, tm=tm, P=P, d=d),
        grid=(n_tiles,), in_specs=in_specs, out_specs=out_specs, out_shape=out_shape,
        compiler_params=pltpu.CompilerParams(dimension_semantics=("arbitrary",), vmem_limit_bytes=VMEM_LIMIT),
        name="mix_a",
    )(h, h, init_xn, lw["mix_pre_g"], lw["w_in"], lw["w_in"], lw["mu_rkv"], lw["mu_wag"],
      lw["w0"], lw["w1"], lw["w2"], lw["a0"], lw["a1"], lw["a2"], lw["g1"], lw["g2"])


def _mix_b_tokens(h, gpre, wu, wvg, wgb, vng, vnb):
    xb = _rms(h, gpre).astype(BF16)
    gv = _gelu(_dot(xb, wvg))
    mean = jnp.mean(gv, axis=-1, keepdims=True)
    cen = gv - mean
    var = jnp.mean(cen * cen, axis=-1, keepdims=True)
    u = _gelu(_dot(xb, wu))
    vn = cen * lax.rsqrt(var + EPS) * vng + vnb
    su = jax.nn.sigmoid(_dot(xb, wgb)) * u
    return su, vn


def _mix_b_prompt_body(h_ref, gpre_ref, wu_ref, wvg_ref, wgb_ref, vng_ref, vnb_ref, ws_ref, bias_ref,
                       ob_ref, vn_scr, su_scr, cm_scr, *, tm, nb):
    s = pl.program_id(1)
    n_sub = pl.num_programs(1)
    su, vn = _mix_b_tokens(h_ref[...], gpre_ref[...], wu_ref[...], wvg_ref[...], wgb_ref[...],
                           vng_ref[...], vnb_ref[...])
    row0 = pl.multiple_of(s * tm, tm)
    for g in range(N_GROUPS_B):
        vn_scr[g, pl.ds(row0, tm), :] = vn[:, g * CHUNK:(g + 1) * CHUNK]
        su_scr[g, pl.ds(row0, tm), :] = su[:, g * CHUNK:(g + 1) * CHUNK]

    @pl.when(s == n_sub - 1)
    def _():
        row = lax.broadcasted_iota(jnp.int32, (CHUNK, CHUNK), 0)
        col = lax.broadcasted_iota(jnp.int32, (CHUNK, CHUNK), 1)
        for g in range(N_GROUPS_B):
            wm = jnp.where(row >= col, ws_ref[g], 0.0).astype(BF16)
            bias = bias_ref[:, g * CHUNK:(g + 1) * CHUNK]
            for b in range(nb):
                rows = pl.ds(b, CHUNK, stride=nb)
                mixed = _dot(wm, vn_scr[g, rows, :].astype(BF16)) + bias
                cm_scr[g, rows, :] = mixed * su_scr[g, rows, :]
        for g in range(N_GROUPS_B):
            ob_ref[:, g * CHUNK:(g + 1) * CHUNK] = cm_scr[g]


def _mix_b_prompt(h, lw, *, nb, tm):
    T, D = h.shape
    rows_per_chunk = CHUNK * nb
    n_chunks = T // rows_per_chunk
    n_sub = rows_per_chunk // tm
    wcol = lambda j: _layer_spec(lw["layer"], D, D, j)
    in_specs = [
        pl.BlockSpec((tm, D), lambda c, s: (c * n_sub + s, 0)),
        _const_spec((1, D)), wcol(3), wcol(4), wcol(6),
        _const_spec((1, D)), _const_spec((1, D)),
        _const_spec((N_GROUPS_B, CHUNK, CHUNK)), _const_spec((CHUNK, D)),
    ]
    slab = pltpu.VMEM((N_GROUPS_B, rows_per_chunk, CHUNK), F32)
    return pl.pallas_call(
        functools.partial(_mix_b_prompt_body, tm=tm, nb=nb),
        grid=(n_chunks, n_sub), in_specs=in_specs,
        out_specs=pl.BlockSpec((rows_per_chunk, D), lambda c, s: (c, 0)),
        out_shape=jax.ShapeDtypeStruct((T, D), F32),
        scratch_shapes=[slab, slab, slab],
        compiler_params=pltpu.CompilerParams(dimension_semantics=("arbitrary", "arbitrary"),
                                             vmem_limit_bytes=VMEM_LIMIT),
        name="mix_b_prompt",
    )(h, lw["mix_pre_g"], lw["w_in"], lw["w_in"], lw["w_in"], lw["vn_g"], lw["vn_b"], lw["w_s"], lw["bias_full"])


def _mix_b_sample_body(ws_ref, bs_ref, h_ref, gpre_ref, wu_ref, wvg_ref, wgb_ref, vng_ref, vnb_ref,
                       ob_ref, vn_ref, *, nb, L):
    su, vn = _mix_b_tokens(h_ref[...], gpre_ref[...], wu_ref[...], wvg_ref[...], wgb_ref[...],
                           vng_ref[...], vnb_ref[...])
    vn_ref[...] = vn
    for t in range(L):
        cols = []
        for g in range(N_GROUPS_B):
            acc = jnp.full((nb, CHUNK), bs_ref[g * L + t], F32)
            for s in range(t + 1):
                acc = acc + ws_ref[(g * L + t) * L + s] * vn[s * nb:(s + 1) * nb, g * CHUNK:(g + 1) * CHUNK]
            cols.append(acc)
        ob_ref[t * nb:(t + 1) * nb, :] = jnp.concatenate(cols, axis=1) * su[t * nb:(t + 1) * nb]


def _mix_b_sample(h, lw, *, nb, L):
    T, D = h.shape
    wcol = lambda j: _layer_spec(lw["layer"], D, D, j)
    smem = pl.BlockSpec(memory_space=pltpu.SMEM)
    in_specs = [smem, smem, _const_spec((T, D)), _const_spec((1, D)), wcol(3), wcol(4), wcol(6),
                _const_spec((1, D)), _const_spec((1, D))]
    full = pl.BlockSpec((T, D), lambda i: (0, 0))
    return pl.pallas_call(
        functools.partial(_mix_b_sample_body, nb=nb, L=L),
        grid=(1,), in_specs=in_specs, out_specs=[full, full],
        out_shape=[jax.ShapeDtypeStruct((T, D), F32)] * 2,
        compiler_params=pltpu.CompilerParams(dimension_semantics=("arbitrary",), vmem_limit_bytes=VMEM_LIMIT),
        name="mix_b_sample",
    )(lw["ws_head"], lw["bs_head"], h, lw["mix_pre_g"], lw["w_in"], lw["w_in"], lw["w_in"], lw["vn_g"], lw["vn_b"])


def _wkv_body(r_ref, k_ref, v_ref, w_ref, a_ref, s0_ref, kk_ref, ka_ref, rk_ref, lg_ref, lb_ref,
              o_ref, st_ref, vec_ref, y_ref, pend_ref, *, TT):
    NA, BB, K2, R = range(N_STAGED)
    HALF = HEAD // 2

    @pl.when(pl.program_id(1) == 0)
    def _():
        st_ref[...] = s0_ref[...]

    def stage(tt, p_prev):
        k = k_ref[tt]
        a = a_ref[tt]
        kk = k * kk_ref[...]
        nrm = jnp.sqrt(jnp.sum(kk * kk, axis=0, keepdims=True))
        kk = kk / jnp.maximum(nrm, 1e-12)
        p = p_prev * w_ref[tt]
        inv_p = 1.0 / p
        vec_ref[tt, NA] = -kk * p_prev
        vec_ref[tt, BB] = kk * a * inv_p
        vec_ref[tt, K2] = k * (1.0 + (a - 1.0) * ka_ref[...]) * inv_p
        vec_ref[tt, R] = r_ref[tt] * p
        return p

    def recur(cur, nxt, sa):
        sa_next = []
        for half in range(2):
            rows = slice(half * HALF, (half + 1) * HALF)
            sa_h = sa[rows]
            v_h = v_ref[cur, rows, :]
            y_h = jnp.zeros((HALF, LANES), F32)
            san_h = jnp.zeros((HALF, LANES), F32)
            for j in range(HEAD):
                jr = slice(j, j + 1)
                sn = st_ref[j, rows, :] + sa_h * vec_ref[cur, BB, jr, :] + v_h * vec_ref[cur, K2, jr, :]
                st_ref[j, rows, :] = sn
                y_h = y_h + sn * vec_ref[cur, R, jr, :]
                san_h = san_h + sn * vec_ref[nxt, NA, jr, :]
            y_ref[cur, rows, :] = y_h
            sa_next.append(san_h)
        return jnp.concatenate(sa_next, axis=0)

    def finish(tt, slot):
        y = y_ref[slot]
        mean = jnp.mean(y, axis=0, keepdims=True)
        cen = y - mean
        var = jnp.mean(cen * cen, axis=0, keepdims=True)
        yn = cen * lax.rsqrt(var + GN_EPS) * lg_ref[...] + lb_ref[...]
        bonus = jnp.sum(vec_ref[slot, R] * vec_ref[slot, K2] * rk_ref[...], axis=0, keepdims=True)
        o_ref[tt] = yn + bonus * v_ref[slot]

    p_end = lax.fori_loop(0, TT, stage, jnp.ones((HEAD, LANES), F32), unroll=min(AUX_UNROLL, TT))
    pend_ref[...] = p_end

    sa0 = jnp.zeros((HEAD, LANES), F32)
    for j in range(HEAD):
        sa0 = sa0 + st_ref[j] * vec_ref[0, NA, j:j + 1, :]

    lax.fori_loop(0, TT, lambda tt, sa: recur(tt, jnp.minimum(tt + 1, TT - 1), sa), sa0)

    for j in range(HEAD):
        st_ref[j] = st_ref[j] * pend_ref[j:j + 1, :]

    def finish_step(tt, carry):
        finish(tt, tt)
        return carry

    lax.fori_loop(0, TT, finish_step, 0, unroll=min(AUX_UNROLL, TT))


def _wkv(r, k, v, w, a, s0, lw, *, L, nb, TT):
    G = nb // SUBLANES
    tile = pl.BlockSpec((TT, None, HEAD, LANES), lambda g, t: (t, g, 0, 0))
    st = pl.BlockSpec((None, HEAD, HEAD, LANES), lambda g, t: (g, 0, 0, 0))
    par = pl.BlockSpec((HEAD, LANES), lambda g, t: (0, 0))
    return pl.pallas_call(
        functools.partial(_wkv_body, TT=TT),
        grid=(G, L // TT),
        in_specs=[tile] * 5 + [st] + [par] * 5,
        out_specs=[tile, st],
        out_shape=[jax.ShapeDtypeStruct((L, G, HEAD, LANES), F32),
                   jax.ShapeDtypeStruct((G, HEAD, HEAD, LANES), F32)],
        scratch_shapes=[pltpu.VMEM((TT, N_STAGED, HEAD, LANES), F32), pltpu.VMEM((TT, HEAD, LANES), F32),
                        pltpu.VMEM((HEAD, LANES), F32)],
        compiler_params=pltpu.CompilerParams(dimension_semantics=("arbitrary", "arbitrary"),
                                             vmem_limit_bytes=VMEM_LIMIT),
        name="wkv",
    )(r, k, v, w, a, s0, lw["kk_l"], lw["ka_l"], lw["rk_l"], lw["lg_l"], lw["lb_l"])


def _post_body(h_ref, o_ref, ga_ref, ob_ref, p_ref, gpost_ref, wout_ref, gfpre_ref, wup_ref, wdown_ref,
               gfpost_ref, wpg_ref, wpe_ref, out_ref, *, tm, nb):
    o = jnp.concatenate([_from_lanes(o_ref[s, g]) for s, g, _ in _slabs(tm, nb)], axis=0)
    mixed = ga_ref[...] * o + ob_ref[...]
    h = h_ref[...] + _rms(_dot(mixed.astype(BF16), wout_ref[...]), gpost_ref[...])
    z = _dot(_rms(h, gfpre_ref[...]).astype(BF16), wup_ref[...])
    z = jnp.square(jnp.maximum(z, 0.0))
    h = h + _rms(_dot(z.astype(BF16), wdown_ref[...]), gfpost_ref[...])
    gate = jax.nn.sigmoid(_dot(h.astype(BF16), wpg_ref[...]))
    out_ref[...] = h + gate * _dot(p_ref[...].astype(BF16), wpe_ref[...])


def _post(h, o, ga, ob, p, lw, *, tm, nb):
    T, D = h.shape
    tok = pl.BlockSpec((tm, D), lambda i: (i, 0))
    lane = pl.BlockSpec((tm // nb, nb // SUBLANES, HEAD, LANES), lambda i: (i, 0, 0, 0))
    in_specs = [tok, lane, tok, tok, pl.BlockSpec((tm, p.shape[1]), lambda i: (i, 0)),
                _const_spec((1, D)), _layer_spec(lw["layer"], *lw["w_out"].shape[1:]),
                _const_spec((1, D)), _layer_spec(lw["layer"], *lw["w_up"].shape[1:]),
                _layer_spec(lw["layer"], *lw["w_down"].shape[1:]),
                _const_spec((1, D)), _layer_spec(lw["layer"], *lw["w_pg"].shape[1:]),
                _layer_spec(lw["layer"], *lw["w_pe"].shape[1:])]
    return pl.pallas_call(
        functools.partial(_post_body, tm=tm, nb=nb), grid=(T // tm,), in_specs=in_specs, out_specs=tok,
        out_shape=jax.ShapeDtypeStruct((T, D), F32),
        compiler_params=pltpu.CompilerParams(dimension_semantics=("arbitrary",), vmem_limit_bytes=VMEM_LIMIT),
        name="post",
    )(h, o, ga, ob, p, lw["mix_post_g"], lw["w_out"], lw["ffn_pre_g"], lw["w_up"], lw["w_down"],
      lw["ffn_post_g"], lw["w_pg"], lw["w_pe"])


def _lane_param(x, n_heads):
    xh = jnp.swapaxes(x.reshape(n_heads // 2, 2, HEAD), 0, 1).reshape(n_heads, HEAD)
    return jnp.repeat(xh.T, SUBLANES, axis=1).astype(F32)


def _state_to_lanes(s):
    nb, nh = s.shape[0], s.shape[1]
    s = s.reshape(nb // SUBLANES, SUBLANES, nh // 2, 2, HEAD, HEAD)
    return jnp.transpose(s, (0, 5, 4, 3, 2, 1)).reshape(nb // SUBLANES, HEAD, HEAD, nh * SUBLANES)


def _state_from_lanes(s, nh):
    G = s.shape[0]
    s = jnp.transpose(s.reshape(G, HEAD, HEAD, 2, nh // 2, SUBLANES), (0, 5, 4, 3, 2, 1))
    return s.reshape(G * SUBLANES, nh, HEAD, HEAD)


def _layer(h, p, s0, init_xn, lw, *, L, nb, prompt):
    D = h.shape[1]
    nh = D // HEAD
    if prompt:
        tm, P = 256, BF16_ROWS
        r, k, v, w, a, ga, last = _mix_a(h, init_xn, lw, tm=tm, P=P, d=nb)
        ob = _mix_b_prompt(h, lw, nb=nb, tm=256)
        vn = None
        TT = WKV_BLOCK
    else:
        r, k, v, w, a, ga, last = _mix_a(h, init_xn, lw, tm=nb, P=nb, d=nb)
        ob, vn = _mix_b_sample(h, lw, nb=nb, L=L)
        TT = L
    o, sT = _wkv(r, k, v, w, a, s0, lw, L=L, nb=nb, TT=TT)
    h = _post(h, o, ga, ob, p, lw, tm=min(256, h.shape[0]), nb=nb)
    return h, _state_from_lanes(sT, nh), last[-nb:], vn


def kernel(x_prompt, x_sample, state_wkv, state_shift, p_prompt, p_sample, mix_pre_g, mix_post_g, ffn_pre_g, ffn_post_g, w_in, mu_rkv, mu_wag, w0, w1, w2, a0, a1, a2, g1, g2, k_k, k_a, r_k, lnx_g, lnx_b, vn_g, vn_b, w_s, b_s, w_out, w_up, w_down, w_pe, w_pg):
    B, L, D = x_prompt.shape
    Bs, Ls, _ = x_sample.shape
    depth = w_in.shape[0]
    nh = D // HEAD
    row = lambda x: x.reshape(1, -1).astype(F32)
    tmajor = lambda x: jnp.swapaxes(x, 0, 1).reshape(-1, x.shape[-1])

    hp = tmajor(x_prompt)
    hs = tmajor(x_sample)
    zero_state = jnp.zeros((B // SUBLANES, HEAD, HEAD, nh * SUBLANES), F32)
    big = dict(w_in=w_in.astype(BF16), w_out=w_out.astype(BF16), w_up=w_up.astype(BF16),
               w_down=w_down.astype(BF16), w_pe=w_pe.astype(BF16), w_pg=w_pg.astype(BF16))
    wkv_p, sh_p, wkv_s, sh_s, cv_s = [], [], [], [], []
    for i in range(depth):
        lw = dict(
            big, layer=i,
            mix_pre_g=row(mix_pre_g[i]), mix_post_g=row(mix_post_g[i]),
            ffn_pre_g=row(ffn_pre_g[i]), ffn_post_g=row(ffn_post_g[i]),
            mu_rkv=row(mu_rkv[i]), mu_wag=mu_wag[i],
            w0=row(w0[i]), w1=w1[i].astype(BF16), w2=w2[i].astype(BF16),
            a0=row(a0[i]), a1=a1[i].astype(BF16), a2=a2[i].astype(BF16),
            g1=g1[i].astype(BF16), g2=g2[i].astype(BF16),
            kk_l=_lane_param(k_k[i], nh), ka_l=_lane_param(k_a[i], nh), rk_l=_lane_param(r_k[i].reshape(-1), nh),
            lg_l=_lane_param(lnx_g[i], nh), lb_l=_lane_param(lnx_b[i], nh),
            vn_g=row(vn_g[i]), vn_b=row(vn_b[i]), w_s=w_s[i],
            bias_full=jnp.repeat(b_s[i].T, CHUNK, axis=1),
            ws_head=w_s[i][:, :Ls, :Ls].reshape(-1), bs_head=b_s[i][:, :Ls].reshape(-1),
        )
        hp, S_p, last_p, _ = _layer(hp, tmajor(p_prompt[i]), zero_state, jnp.zeros((BF16_ROWS, D), F32), lw,
                                    L=L, nb=B, prompt=True)
        hs, S_s, last_s, vn_s = _layer(hs, tmajor(p_sample[i]), _state_to_lanes(state_wkv[i]), state_shift[i], lw,
                                       L=Ls, nb=Bs, prompt=False)
        wkv_p.append(S_p)
        sh_p.append(last_p)
        wkv_s.append(S_s)
        sh_s.append(last_s)
        cv_s.append(jnp.swapaxes(vn_s.reshape(Ls, Bs, D), 0, 1))
    yp = jnp.swapaxes(hp.reshape(L, B, D), 0, 1)
    ys = jnp.swapaxes(hs.reshape(Ls, Bs, D), 0, 1)
    return (yp, ys, jnp.stack(wkv_p), jnp.stack(sh_p), jnp.stack(wkv_s), jnp.stack(sh_s), jnp.stack(cv_s))
```

```python
import functools

import numpy as np
import jax
import jax.numpy as jnp
from jax import lax
from jax.experimental import pallas as pl
from jax.experimental.pallas import tpu as pltpu

F32 = jnp.float32
BF16 = jnp.bfloat16

HEAD = 64
CHUNK = 128
N_GROUPS_B = 8
EPS = 1e-6
GN_EPS = 64e-5
SQRT_HALF = float(np.sqrt(0.5))
EXP_NEG_HALF = float(np.exp(-0.5))

LANES = 128
SUBLANES = 8
BF16_ROWS = 16
AUX_UNROLL = 4
N_STAGED = 4
WKV_BLOCK = 32
VMEM_LIMIT = 56 * 1024 * 1024


def _const_spec(shape):
    nd = len(shape)
    return pl.BlockSpec(shape, lambda *_: (0,) * nd, pipeline_mode=pl.Buffered(1))


def _layer_spec(layer, rows, cols, col_block=0):
    return pl.BlockSpec((None, rows, cols), lambda *_: (layer, 0, col_block), pipeline_mode=pl.Buffered(1))


def _rms(x, g):
    ms = jnp.mean(x * x, axis=-1, keepdims=True)
    return x * lax.rsqrt(ms + EPS) * g


def _gelu(x):
    return 0.5 * x * (1.0 + lax.erf(x * SQRT_HALF))


def _dot(a, b):
    return jnp.dot(a, b, preferred_element_type=F32)


def _lane_head_order(n_heads):
    return [2 * blk + par for par in range(2) for blk in range(n_heads // 2)]


def _to_lanes(x):
    heads = _lane_head_order(x.shape[1] // HEAD)
    return jnp.concatenate([x[:, h * HEAD:(h + 1) * HEAD] for h in heads], axis=0).T


def _swap_head_pairs(x):
    shape = x.shape
    return jnp.flip(x.reshape(shape[:-1] + (shape[-1] // LANES, 2, HEAD)), axis=-2).reshape(shape)


def _to_lanes2(x1, x2s):
    low = lax.broadcasted_iota(jnp.int32, (SUBLANES, LANES), 1) < HEAD
    first, second = [], []
    for blk in range(x1.shape[1] // LANES):
        a1 = x1[:, blk * LANES:(blk + 1) * LANES]
        a2 = x2s[:, blk * LANES:(blk + 1) * LANES]
        first.append(jnp.where(low, a1, a2))
        second.append(jnp.where(low, a2, a1))
    zt = jnp.concatenate(first + second, axis=0).T
    top, bottom = zt[:HEAD], zt[HEAD:]
    low = lax.broadcasted_iota(jnp.int32, (HEAD, LANES), 1) < HEAD
    return jnp.where(low, top, bottom), jnp.where(low, bottom, top)


def _from_lanes(y):
    n_heads = LANES // SUBLANES
    z = y.T
    blocks = [None] * n_heads
    for pos, h in enumerate(_lane_head_order(n_heads)):
        blocks[h] = z[pos * SUBLANES:(pos + 1) * SUBLANES, :]
    return jnp.concatenate(blocks, axis=1)


def _slabs(tm, nb):
    tg = nb // SUBLANES
    return [(s, g, (s * tg + g) * SUBLANES) for s in range(tm // nb) for g in range(tg)]


def _mix_a_body(h_ref, hp_ref, init_ref, gpre_ref, wrkv_ref, wga_ref, mu_rkv_ref, mu_wag_ref,
                w0_ref, w1_ref, w2_ref, a0_ref, a1_ref, a2_ref, g1_ref, g2_ref,
                r_ref, k_ref, v_ref, w_ref, a_ref, ga_ref, last_ref, *, tm, P, d):
    D = h_ref.shape[-1]
    slabs = _slabs(tm, d)
    i = pl.program_id(0)
    g = gpre_ref[...]
    xn = _rms(h_ref[...], g)
    prev = jnp.where(i == 0, init_ref[...], _rms(hp_ref[...], g))
    xn_ext = jnp.concatenate([prev, xn], axis=0)
    xs = xn_ext[P - d:P - d + tm]
    dx = xs - xn
    mw = mu_wag_ref[...]
    xw = xn + dx * mw[0:1]
    xa = xn + dx * mw[1:2]
    xg = xn + dx * mw[2:3]
    xb_ext = xn_ext.astype(BF16)
    xb = xb_ext[P:]
    mu = mu_rkv_ref[...]

    def shifted_proj(s, lhs):
        pe = _dot(lhs, wrkv_ref[:, s * D:(s + 1) * D])
        cur = pe[P:]
        prv = pe[P - d:P - d + tm]
        return cur + (prv - cur) * mu[:, s * D:(s + 1) * D]

    def exact_zero(tiles):
        acc = jnp.zeros((SUBLANES, LANES), jnp.uint32)
        for t in tiles:
            acc = acc | lax.bitcast_convert_type(t[:SUBLANES], jnp.uint32)
        acc = lax.shift_right_logical(lax.shift_right_logical(acc, jnp.uint32(16)), jnp.uint32(16))
        return lax.bitcast_convert_type(acc, F32)

    def put_pair(x1, x2s, out1, out2):
        tiles = []
        for s, g, r0 in slabs:
            t1, t2 = _to_lanes2(x1[r0:r0 + SUBLANES], x2s[r0:r0 + SUBLANES])
            out1[s, g], out2[s, g] = t1, t2
            tiles += [t1, t2]
        return exact_zero(tiles)

    def after(x, zero):
        rows = BF16_ROWS if x.dtype == BF16 else SUBLANES
        z = jnp.concatenate([zero.astype(x.dtype)] * (rows // SUBLANES), axis=0)
        z = jnp.concatenate([z] * (x.shape[1] // LANES), axis=1)
        return x + jnp.concatenate([z] * (x.shape[0] // rows), axis=0)

    lw1 = jnp.tanh(_dot(xw.astype(BF16), w1_ref[...])).astype(BF16)
    la1 = _dot(xa.astype(BF16), a1_ref[...]).astype(BF16)
    lg1 = jax.nn.sigmoid(_dot(xg.astype(BF16), g1_ref[...])).astype(BF16)
    v = shifted_proj(2, xb_ext)
    r = shifted_proj(0, xb_ext)
    wl = w0_ref[...] + _dot(lw1, w2_ref[...])
    w = jnp.exp(-EXP_NEG_HALF * jax.nn.sigmoid(wl))
    a_s = jax.nn.sigmoid(a0_ref[...] + _dot(la1, a2_ref[...]))
    v_tiles = []
    for s, g, r0 in slabs:
        v_tiles.append(_to_lanes(v[r0:r0 + SUBLANES]))
        v_ref[s, g] = v_tiles[-1]
    k_s = shifted_proj(1, after(xb_ext, exact_zero(v_tiles)))
    z_wa = put_pair(w, a_s, w_ref, a_ref)
    gate = _dot(after(xb, z_wa), wga_ref[...])
    z_rk = put_pair(r, k_s, r_ref, k_ref)
    ga_ref[...] = after(jax.nn.sigmoid(gate) * _dot(lg1, g2_ref[...]), z_rk)
    last_ref[...] = xn[tm - P:]


def _mix_a(h, init_xn, lw, *, tm, P, d):
    T, D = h.shape
    n_tiles = T // tm
    ratio = tm // P
    tok = pl.BlockSpec((tm, D), lambda i: (i, 0))
    in_specs = [
        tok,
        pl.BlockSpec((P, D), lambda i: (jnp.maximum(i * ratio - 1, 0), 0)),
        _const_spec((P, D)),
        _const_spec((1, D)),
        _layer_spec(lw["layer"], D, 3 * D, 0),
        _layer_spec(lw["layer"], D, D, 5),
        _const_spec((1, 3 * D)), _const_spec((3, D)),
        _const_spec((1, D)), _const_spec(lw["w1"].shape), _const_spec(lw["w2"].shape),
        _const_spec((1, D)), _const_spec(lw["a1"].shape), _const_spec(lw["a2"].shape),
        _const_spec(lw["g1"].shape), _const_spec(lw["g2"].shape),
    ]
    ts, tg = tm // d, d // SUBLANES
    lane = pl.BlockSpec((ts, tg, HEAD, LANES), lambda i: (i, 0, 0, 0))
    lane_shape = jax.ShapeDtypeStruct((T // d, tg, HEAD, LANES), F32)
    out_specs = [lane] * 5 + [tok, pl.BlockSpec((P, D), lambda i: (0, 0))]
    out_shape = [lane_shape] * 5 + [jax.ShapeDtypeStruct((T, D), F32), jax.ShapeDtypeStruct((P, D), F32)]
    return pl.pallas_call(
        functools.partial(_mix_a_body, tm=tm, P=P, d=d),
        grid=(n_tiles,), in_specs=in_specs, out_specs=out_specs, out_shape=out_shape,
        compiler_params=pltpu.CompilerParams(dimension_semantics=("arbitrary",), vmem_limit_bytes=VMEM_LIMIT),
        name="mix_a",
    )(h, h, init_xn, lw["mix_pre_g"], lw["w_in"], lw["w_in"], lw["mu_rkv"], lw["mu_wag"],
      lw["w0"], lw["w1"], lw["w2"], lw["a0"], lw["a1"], lw["a2"], lw["g1"], lw["g2"])


def _mix_b_tokens(h, gpre, wu, wvg, wgb, vng, vnb):
    xb = _rms(h, gpre).astype(BF16)
    gv = _gelu(_dot(xb, wvg))
    mean = jnp.mean(gv, axis=-1, keepdims=True)
    cen = gv - mean
    var = jnp.mean(cen * cen, axis=-1, keepdims=True)
    u = _gelu(_dot(xb, wu))
    vn = cen * lax.rsqrt(var + EPS) * vng + vnb
    su = jax.nn.sigmoid(_dot(xb, wgb)) * u
    return su, vn


def _mix_b_prompt_body(h_ref, gpre_ref, wu_ref, wvg_ref, wgb_ref, vng_ref, vnb_ref, ws_ref, bias_ref,
                       ob_ref, vn_scr, su_scr, cm_scr, *, tm, nb):
    s = pl.program_id(1)
    n_sub = pl.num_programs(1)
    su, vn = _mix_b_tokens(h_ref[...], gpre_ref[...], wu_ref[...], wvg_ref[...], wgb_ref[...],
                           vng_ref[...], vnb_ref[...])
    row0 = pl.multiple_of(s * tm, tm)
    for g in range(N_GROUPS_B):
        vn_scr[g, pl.ds(row0, tm), :] = vn[:, g * CHUNK:(g + 1) * CHUNK]
        su_scr[g, pl.ds(row0, tm), :] = su[:, g * CHUNK:(g + 1) * CHUNK]

    @pl.when(s == n_sub - 1)
    def _():
        row = lax.broadcasted_iota(jnp.int32, (CHUNK, CHUNK), 0)
        col = lax.broadcasted_iota(jnp.int32, (CHUNK, CHUNK), 1)
        for g in range(N_GROUPS_B):
            wm = jnp.where(row >= col, ws_ref[g], 0.0).astype(BF16)
            bias = bias_ref[:, g * CHUNK:(g + 1) * CHUNK]
            for b in range(nb):
                rows = pl.ds(b, CHUNK, stride=nb)
                mixed = _dot(wm, vn_scr[g, rows, :].astype(BF16)) + bias
                cm_scr[g, rows, :] = mixed * su_scr[g, rows, :]
        for g in range(N_GROUPS_B):
            ob_ref[:, g * CHUNK:(g + 1) * CHUNK] = cm_scr[g]


def _mix_b_prompt(h, lw, *, nb, tm):
    T, D = h.shape
    rows_per_chunk = CHUNK * nb
    n_chunks = T // rows_per_chunk
    n_sub = rows_per_chunk // tm
    wcol = lambda j: _layer_spec(lw["layer"], D, D, j)
    in_specs = [
        pl.BlockSpec((tm, D), lambda c, s: (c * n_sub + s, 0)),
        _const_spec((1, D)), wcol(3), wcol(4), wcol(6),
        _const_spec((1, D)), _const_spec((1, D)),
        _const_spec((N_GROUPS_B, CHUNK, CHUNK)), _const_spec((CHUNK, D)),
    ]
    slab = pltpu.VMEM((N_GROUPS_B, rows_per_chunk, CHUNK), F32)
    return pl.pallas_call(
        functools.partial(_mix_b_prompt_body, tm=tm, nb=nb),
        grid=(n_chunks, n_sub), in_specs=in_specs,
        out_specs=pl.BlockSpec((rows_per_chunk, D), lambda c, s: (c, 0)),
        out_shape=jax.ShapeDtypeStruct((T, D), F32),
        scratch_shapes=[slab, slab, slab],
        compiler_params=pltpu.CompilerParams(dimension_semantics=("arbitrary", "arbitrary"),
                                             vmem_limit_bytes=VMEM_LIMIT),
        name="mix_b_prompt",
    )(h, lw["mix_pre_g"], lw["w_in"], lw["w_in"], lw["w_in"], lw["vn_g"], lw["vn_b"], lw["w_s"], lw["bias_full"])


def _mix_b_sample_body(ws_ref, bs_ref, h_ref, gpre_ref, wu_ref, wvg_ref, wgb_ref, vng_ref, vnb_ref,
                       ob_ref, vn_ref, *, nb, L):
    su, vn = _mix_b_tokens(h_ref[...], gpre_ref[...], wu_ref[...], wvg_ref[...], wgb_ref[...],
                           vng_ref[...], vnb_ref[...])
    vn_ref[...] = vn
    for t in range(L):
        cols = []
        for g in range(N_GROUPS_B):
            acc = jnp.full((nb, CHUNK), bs_ref[g * L + t], F32)
            for s in range(t + 1):
                acc = acc + ws_ref[(g * L + t) * L + s] * vn[s * nb:(s + 1) * nb, g * CHUNK:(g + 1) * CHUNK]
            cols.append(acc)
        ob_ref[t * nb:(t + 1) * nb, :] = jnp.concatenate(cols, axis=1) * su[t * nb:(t + 1) * nb]


def _mix_b_sample(h, lw, *, nb, L):
    T, D = h.shape
    wcol = lambda j: _layer_spec(lw["layer"], D, D, j)
    smem = pl.BlockSpec(memory_space=pltpu.SMEM)
    in_specs = [smem, smem, _const_spec((T, D)), _const_spec((1, D)), wcol(3), wcol(4), wcol(6),
                _const_spec((1, D)), _const_spec((1, D))]
    full = pl.BlockSpec((T, D), lambda i: (0, 0))
    return pl.pallas_call(
        functools.partial(_mix_b_sample_body, nb=nb, L=L),
        grid=(1,), in_specs=in_specs, out_specs=[full, full],
        out_shape=[jax.ShapeDtypeStruct((T, D), F32)] * 2,
        compiler_params=pltpu.CompilerParams(dimension_semantics=("arbitrary",), vmem_limit_bytes=VMEM_LIMIT),
        name="mix_b_sample",
    )(lw["ws_head"], lw["bs_head"], h, lw["mix_pre_g"], lw["w_in"], lw["w_in"], lw["w_in"], lw["vn_g"], lw["vn_b"])


def _wkv_body(r_ref, k_ref, v_ref, w_ref, a_ref, s0_ref, kk_ref, ka_ref, rk_ref, lg_ref, lb_ref,
              o_ref, st_ref, vec_ref, y_ref, pend_ref, *, TT):
    NA, BB, K2, R = range(N_STAGED)
    HALF = HEAD // 2

    @pl.when(pl.program_id(1) == 0)
    def _():
        st_ref[...] = s0_ref[...]

    def stage(tt, p_prev):
        k = k_ref[tt]
        a = a_ref[tt]
        kk = k * kk_ref[...]
        nrm = jnp.sqrt(jnp.sum(kk * kk, axis=0, keepdims=True))
        kk = kk / jnp.maximum(nrm, 1e-12)
        p = p_prev * w_ref[tt]
        inv_p = 1.0 / p
        vec_ref[tt, NA] = -kk * p_prev
        vec_ref[tt, BB] = kk * a * inv_p
        vec_ref[tt, K2] = k * (1.0 + (a - 1.0) * ka_ref[...]) * inv_p
        vec_ref[tt, R] = r_ref[tt] * p
        return p

    def recur(cur, nxt, sa):
        sa_next = []
        for half in range(2):
            rows = slice(half * HALF, (half + 1) * HALF)
            sa_h = sa[rows]
            v_h = v_ref[cur, rows, :]
            y_h = jnp.zeros((HALF, LANES), F32)
            san_h = jnp.zeros((HALF, LANES), F32)
            for j in range(HEAD):
                jr = slice(j, j + 1)
                sn = st_ref[j, rows, :] + sa_h * vec_ref[cur, BB, jr, :] + v_h * vec_ref[cur, K2, jr, :]
                st_ref[j, rows, :] = sn
                y_h = y_h + sn * vec_ref[cur, R, jr, :]
                san_h = san_h + sn * vec_ref[nxt, NA, jr, :]
            y_ref[cur, rows, :] = y_h
            sa_next.append(san_h)
        return jnp.concatenate(sa_next, axis=0)

    def finish(tt, slot):
        y = y_ref[slot]
        mean = jnp.mean(y, axis=0, keepdims=True)
        cen = y - mean
        var = jnp.mean(cen * cen, axis=0, keepdims=True)
        yn = cen * lax.rsqrt(var + GN_EPS) * lg_ref[...] + lb_ref[...]
        bonus = jnp.sum(vec_ref[slot, R] * vec_ref[slot, K2] * rk_ref[...], axis=0, keepdims=True)
        o_ref[tt] = yn + bonus * v_ref[slot]

    p_end = lax.fori_loop(0, TT, stage, jnp.ones((HEAD, LANES), F32), unroll=min(AUX_UNROLL, TT))
    pend_ref[...] = p_end

    sa0 = jnp.zeros((HEAD, LANES), F32)
    for j in range(HEAD):
        sa0 = sa0 + st_ref[j] * vec_ref[0, NA, j:j + 1, :]

    lax.fori_loop(0, TT, lambda tt, sa: recur(tt, jnp.minimum(tt + 1, TT - 1), sa), sa0)

    for j in range(HEAD):
        st_ref[j] = st_ref[j] * pend_ref[j:j + 1, :]

    def finish_step(tt, carry):
        finish(tt, tt)
        return carry

    lax.fori_loop(0, TT, finish_step, 0, unroll=min(AUX_UNROLL, TT))


def _wkv(r, k, v, w, a, s0, lw, *, L, nb, TT):
    G = nb // SUBLANES
    tile = pl.BlockSpec((TT, None, HEAD, LANES), lambda g, t: (t, g, 0, 0))
    st = pl.BlockSpec((None, HEAD, HEAD, LANES), lambda g, t: (g, 0, 0, 0))
    par = pl.BlockSpec((HEAD, LANES), lambda g, t: (0, 0))
    return pl.pallas_call(
        functools.partial(_wkv_body, TT=TT),
        grid=(G, L // TT),
        in_specs=[tile] * 5 + [st] + [par] * 5,
        out_specs=[tile, st],
        out_shape=[jax.ShapeDtypeStruct((L, G, HEAD, LANES), F32),
                   jax.ShapeDtypeStruct((G, HEAD, HEAD, LANES), F32)],
        scratch_shapes=[pltpu.VMEM((TT, N_STAGED, HEAD, LANES), F32), pltpu.VMEM((TT, HEAD, LANES), F32),
                        pltpu.VMEM((HEAD, LANES), F32)],
        compiler_params=pltpu.CompilerParams(dimension_semantics=("arbitrary", "arbitrary"),
                                             vmem_limit_bytes=VMEM_LIMIT),
        name="wkv",
    )(r, k, v, w, a, s0, lw["kk_l"], lw["ka_l"], lw["rk_l"], lw["lg_l"], lw["lb_l"])


def _post_body(h_ref, o_ref, ga_ref, ob_ref, p_ref, gpost_ref, wout_ref, gfpre_ref, wup_ref, wdown_ref,
               gfpost_ref, wpg_ref, wpe_ref, out_ref, *, tm, nb):
    o = jnp.concatenate([_from_lanes(o_ref[s, g]) for s, g, _ in _slabs(tm, nb)], axis=0)
    mixed = ga_ref[...] * o + ob_ref[...]
    h = h_ref[...] + _rms(_dot(mixed.astype(BF16), wout_ref[...]), gpost_ref[...])
    z = _dot(_rms(h, gfpre_ref[...]).astype(BF16), wup_ref[...])
    z = jnp.square(jnp.maximum(z, 0.0))
    h = h + _rms(_dot(z.astype(BF16), wdown_ref[...]), gfpost_ref[...])
    gate = jax.nn.sigmoid(_dot(h.astype(BF16), wpg_ref[...]))
    out_ref[...] = h + gate * _dot(p_ref[...].astype(BF16), wpe_ref[...])


def _post(h, o, ga, ob, p, lw, *, tm, nb):
    T, D = h.shape
    tok = pl.BlockSpec((tm, D), lambda i: (i, 0))
    lane = pl.BlockSpec((tm // nb, nb // SUBLANES, HEAD, LANES), lambda i: (i, 0, 0, 0))
    in_specs = [tok, lane, tok, tok, pl.BlockSpec((tm, p.shape[1]), lambda i: (i, 0)),
                _const_spec((1, D)), _layer_spec(lw["layer"], *lw["w_out"].shape[1:]),
                _const_spec((1, D)), _layer_spec(lw["layer"], *lw["w_up"].shape[1:]),
                _layer_spec(lw["layer"], *lw["w_down"].shape[1:]),
                _const_spec((1, D)), _layer_spec(lw["layer"], *lw["w_pg"].shape[1:]),
                _layer_spec(lw["layer"], *lw["w_pe"].shape[1:])]
    return pl.pallas_call(
        functools.partial(_post_body, tm=tm, nb=nb), grid=(T // tm,), in_specs=in_specs, out_specs=tok,
        out_shape=jax.ShapeDtypeStruct((T, D), F32),
        compiler_params=pltpu.CompilerParams(dimension_semantics=("arbitrary",), vmem_limit_bytes=VMEM_LIMIT),
        name="post",
    )(h, o, ga, ob, p, lw["mix_post_g"], lw["w_out"], lw["ffn_pre_g"], lw["w_up"], lw["w_down"],
      lw["ffn_post_g"], lw["w_pg"], lw["w_pe"])


def _lane_param(x, n_heads):
    xh = jnp.swapaxes(x.reshape(n_heads // 2, 2, HEAD), 0, 1).reshape(n_heads, HEAD)
    return jnp.repeat(xh.T, SUBLANES, axis=1).astype(F32)


def _state_to_lanes(s):
    nb, nh = s.shape[0], s.shape[1]
    s = s.reshape(nb // SUBLANES, SUBLANES, nh // 2, 2, HEAD, HEAD)
    return jnp.transpose(s, (0, 5, 4, 3, 2, 1)).reshape(nb // SUBLANES, HEAD, HEAD, nh * SUBLANES)


def _state_from_lanes(s, nh):
    G = s.shape[0]
    s = jnp.transpose(s.reshape(G, HEAD, HEAD, 2, nh // 2, SUBLANES), (0, 5, 4, 3, 2, 1))
    return s.reshape(G * SUBLANES, nh, HEAD, HEAD)


def _layer(h, p, s0, init_xn, lw, *, L, nb, prompt):
    D = h.shape[1]
    nh = D // HEAD
    if prompt:
        tm, P = 256, BF16_ROWS
        r, k, v, w, a, ga, last = _mix_a(h, init_xn, lw, tm=tm, P=P, d=nb)
        ob = _mix_b_prompt(h, lw, nb=nb, tm=256)
        vn = None
        TT = WKV_BLOCK
    else:
        r, k, v, w, a, ga, last = _mix_a(h, init_xn, lw, tm=nb, P=nb, d=nb)
        ob, vn = _mix_b_sample(h, lw, nb=nb, L=L)
        TT = L
    o, sT = _wkv(r, k, v, w, a, s0, lw, L=L, nb=nb, TT=TT)
    h = _post(h, o, ga, ob, p, lw, tm=min(256, h.shape[0]), nb=nb)
    return h, _state_from_lanes(sT, nh), last[-nb:], vn


def kernel(x_prompt, x_sample, state_wkv, state_shift, p_prompt, p_sample, mix_pre_g, mix_post_g, ffn_pre_g, ffn_post_g, w_in, mu_rkv, mu_wag, w0, w1, w2, a0, a1, a2, g1, g2, k_k, k_a, r_k, lnx_g, lnx_b, vn_g, vn_b, w_s, b_s, w_out, w_up, w_down, w_pe, w_pg):
    B, L, D = x_prompt.shape
    Bs, Ls, _ = x_sample.shape
    depth = w_in.shape[0]
    nh = D // HEAD
    row = lambda x: x.reshape(1, -1).astype(F32)
    tmajor = lambda x: jnp.swapaxes(x, 0, 1).reshape(-1, x.shape[-1])

    hp = tmajor(x_prompt)
    hs = tmajor(x_sample)
    zero_state = jnp.zeros((B // SUBLANES, HEAD, HEAD, nh * SUBLANES), F32)
    swap_k = lambda x: jnp.concatenate([x[..., :D], _swap_head_pairs(x[..., D:2 * D]), x[..., 2 * D:]], axis=-1)
    big = dict(w_in=swap_k(w_in.astype(BF16)), w_out=w_out.astype(BF16), w_up=w_up.astype(BF16),
               w_down=w_down.astype(BF16), w_pe=w_pe.astype(BF16), w_pg=w_pg.astype(BF16))
    wkv_p, sh_p, wkv_s, sh_s, cv_s = [], [], [], [], []
    for i in range(depth):
        lw = dict(
            big, layer=i,
            mix_pre_g=row(mix_pre_g[i]), mix_post_g=row(mix_post_g[i]),
            ffn_pre_g=row(ffn_pre_g[i]), ffn_post_g=row(ffn_post_g[i]),
            mu_rkv=swap_k(row(mu_rkv[i])), mu_wag=mu_wag[i],
            w0=row(w0[i]), w1=w1[i].astype(BF16), w2=w2[i].astype(BF16),
            a0=_swap_head_pairs(row(a0[i])), a1=a1[i].astype(BF16), a2=_swap_head_pairs(a2[i]).astype(BF16),
            g1=g1[i].astype(BF16), g2=g2[i].astype(BF16),
            kk_l=_lane_param(k_k[i], nh), ka_l=_lane_param(k_a[i], nh), rk_l=_lane_param(r_k[i].reshape(-1), nh),
            lg_l=_lane_param(lnx_g[i], nh), lb_l=_lane_param(lnx_b[i], nh),
            vn_g=row(vn_g[i]), vn_b=row(vn_b[i]), w_s=w_s[i],
            bias_full=jnp.repeat(b_s[i].T, CHUNK, axis=1),
            ws_head=w_s[i][:, :Ls, :Ls].reshape(-1), bs_head=b_s[i][:, :Ls].reshape(-1),
        )
        hp, S_p, last_p, _ = _layer(hp, tmajor(p_prompt[i]), zero_state, jnp.zeros((BF16_ROWS, D), F32), lw,
                                    L=L, nb=B, prompt=True)
        hs, S_s, last_s, vn_s = _layer(hs, tmajor(p_sample[i]), _state_to_lanes(state_wkv[i]), state_shift[i], lw,
                                       L=Ls, nb=Bs, prompt=False)
        wkv_p.append(S_p)
        sh_p.append(last_p)
        wkv_s.append(S_s)
        sh_s.append(last_s)
        cv_s.append(jnp.swapaxes(vn_s.reshape(Ls, Bs, D), 0, 1))
    yp = jnp.swapaxes(hp.reshape(L, B, D), 0, 1)
    ys = jnp.swapaxes(hs.reshape(Ls, Bs, D), 0, 1)
    return (yp, ys, jnp.stack(wkv_p), jnp.stack(sh_p), jnp.stack(wkv_s), jnp.stack(sh_s), jnp.stack(cv_s))
```

```python
import functools

import numpy as np
import jax
import jax.numpy as jnp
from jax import lax
from jax.experimental import pallas as pl
from jax.experimental.pallas import tpu as pltpu

F32 = jnp.float32
BF16 = jnp.bfloat16

HEAD = 64
CHUNK = 128
N_GROUPS_B = 8
EPS = 1e-6
GN_EPS = 64e-5
SQRT_HALF = float(np.sqrt(0.5))
EXP_NEG_HALF = float(np.exp(-0.5))

LANES = 128
SUBLANES = 8
BF16_ROWS = 16
AUX_UNROLL = 4
N_STAGED = 4
WKV_BLOCK = 32
VMEM_LIMIT = 56 * 1024 * 1024


def _const_spec(shape):
    nd = len(shape)
    return pl.BlockSpec(shape, lambda *_: (0,) * nd, pipeline_mode=pl.Buffered(1))


def _layer_spec(layer, rows, cols, col_block=0):
    return pl.BlockSpec((None, rows, cols), lambda *_: (layer, 0, col_block), pipeline_mode=pl.Buffered(1))


def _rms(x, g):
    ms = jnp.mean(x * x, axis=-1, keepdims=True)
    return x * lax.rsqrt(ms + EPS) * g


def _gelu(x):
    return 0.5 * x * (1.0 + lax.erf(x * SQRT_HALF))


def _dot(a, b):
    return jnp.dot(a, b, preferred_element_type=F32)


def _lane_head_order(n_heads):
    return [2 * blk + par for par in range(2) for blk in range(n_heads // 2)]


def _to_lanes(x):
    heads = _lane_head_order(x.shape[1] // HEAD)
    return jnp.concatenate([x[:, h * HEAD:(h + 1) * HEAD] for h in heads], axis=0).T


def _swap_head_pairs(x):
    shape = x.shape
    return jnp.flip(x.reshape(shape[:-1] + (shape[-1] // LANES, 2, HEAD)), axis=-2).reshape(shape)


def _to_lanes2(x1, x2s):
    low = lax.broadcasted_iota(jnp.int32, (SUBLANES, LANES), 1) < HEAD
    first, second = [], []
    for blk in range(x1.shape[1] // LANES):
        a1 = x1[:, blk * LANES:(blk + 1) * LANES]
        a2 = x2s[:, blk * LANES:(blk + 1) * LANES]
        first.append(jnp.where(low, a1, a2))
        second.append(jnp.where(low, a2, a1))
    zt = jnp.concatenate(first + second, axis=0).T
    top, bottom = zt[:HEAD], zt[HEAD:]
    low = lax.broadcasted_iota(jnp.int32, (HEAD, LANES), 1) < HEAD
    return jnp.where(low, top, bottom), jnp.where(low, bottom, top)


def _from_lanes(y):
    n_heads = LANES // SUBLANES
    z = y.T
    blocks = [None] * n_heads
    for pos, h in enumerate(_lane_head_order(n_heads)):
        blocks[h] = z[pos * SUBLANES:(pos + 1) * SUBLANES, :]
    return jnp.concatenate(blocks, axis=1)


def _exact_zero(tiles):
    acc = jnp.zeros((SUBLANES, LANES), jnp.uint32)
    for t in tiles:
        acc = acc | lax.bitcast_convert_type(t[:SUBLANES, :LANES], jnp.uint32)
    acc = lax.shift_right_logical(lax.shift_right_logical(acc, jnp.uint32(16)), jnp.uint32(16))
    return lax.bitcast_convert_type(acc, F32)


def _after(x, zero):
    rows = BF16_ROWS if x.dtype == BF16 else SUBLANES
    z = jnp.concatenate([zero.astype(x.dtype)] * (rows // SUBLANES), axis=0)
    z = jnp.concatenate([z] * (x.shape[1] // LANES), axis=1)
    return x + jnp.concatenate([z] * (x.shape[0] // rows), axis=0)


def _slabs(tm, nb):
    tg = nb // SUBLANES
    return [(s, g, (s * tg + g) * SUBLANES) for s in range(tm // nb) for g in range(tg)]


def _mix_a_body(h_ref, hp_ref, init_ref, gpre_ref, wr_ref, wk_ref, wv_ref, wga_ref, mu_rkv_ref, mu_wag_ref,
                w0_ref, w1_ref, w2_ref, a0_ref, a1_ref, a2_ref, g1_ref, g2_ref,
                r_ref, k_ref, v_ref, w_ref, a_ref, ga_ref, last_ref, *, tm, P, d):
    D = h_ref.shape[-1]
    slabs = _slabs(tm, d)
    i = pl.program_id(0)
    g = gpre_ref[...]
    xn = _rms(h_ref[...], g)
    prev = jnp.where(i == 0, init_ref[...], _rms(hp_ref[...], g))
    xn_ext = jnp.concatenate([prev, xn], axis=0)
    xs = xn_ext[P - d:P - d + tm]
    dx = xs - xn
    mw = mu_wag_ref[...]
    xw = xn + dx * mw[0:1]
    xa = xn + dx * mw[1:2]
    xg = xn + dx * mw[2:3]
    xb_ext = xn_ext.astype(BF16)
    xb = xb_ext[P:]
    mu = mu_rkv_ref[...]

    def shifted_proj(s, lhs):
        pe = _dot(lhs, (wr_ref, wk_ref, wv_ref)[s][...])
        cur = pe[P:]
        prv = pe[P - d:P - d + tm]
        return cur + (prv - cur) * mu[:, s * D:(s + 1) * D]

    exact_zero, after = _exact_zero, _after

    def put_pair(x1, x2s, out1, out2):
        tiles = []
        for s, g, r0 in slabs:
            t1, t2 = _to_lanes2(x1[r0:r0 + SUBLANES], x2s[r0:r0 + SUBLANES])
            out1[s, g], out2[s, g] = t1, t2
            tiles += [t1, t2]
        return exact_zero(tiles)

    lw1 = jnp.tanh(_dot(xw.astype(BF16), w1_ref[...])).astype(BF16)
    la1 = _dot(xa.astype(BF16), a1_ref[...]).astype(BF16)
    lg1 = jax.nn.sigmoid(_dot(xg.astype(BF16), g1_ref[...])).astype(BF16)
    v = shifted_proj(2, xb_ext)
    r = shifted_proj(0, xb_ext)
    wl = w0_ref[...] + _dot(lw1, w2_ref[...])
    w = jnp.exp(-EXP_NEG_HALF * jax.nn.sigmoid(wl))
    a_s = jax.nn.sigmoid(a0_ref[...] + _dot(la1, a2_ref[...]))
    v_tiles = []
    for s, g, r0 in slabs:
        v_tiles.append(_to_lanes(v[r0:r0 + SUBLANES]))
        v_ref[s, g] = v_tiles[-1]
    k_s = shifted_proj(1, after(xb_ext, exact_zero(v_tiles)))
    z_wa = put_pair(w, a_s, w_ref, a_ref)
    gate = _dot(after(xb, z_wa), wga_ref[...])
    z_rk = put_pair(r, k_s, r_ref, k_ref)
    ga_ref[...] = after(jax.nn.sigmoid(gate) * _dot(lg1, g2_ref[...]), z_rk)
    last_ref[...] = xn[tm - P:]


def _mix_a(h, init_xn, lw, *, tm, P, d):
    T, D = h.shape
    n_tiles = T // tm
    ratio = tm // P
    tok = pl.BlockSpec((tm, D), lambda i: (i, 0))
    in_specs = [
        tok,
        pl.BlockSpec((P, D), lambda i: (jnp.maximum(i * ratio - 1, 0), 0)),
        _const_spec((P, D)),
        _const_spec((1, D)),
        _layer_spec(lw["layer"], D, D, 0),
        _layer_spec(lw["layer"], D, D, 0),
        _layer_spec(lw["layer"], D, D, 2),
        _layer_spec(lw["layer"], D, D, 5),
        _const_spec((1, 3 * D)), _const_spec((3, D)),
        _const_spec((1, D)), _const_spec(lw["w1"].shape), _const_spec(lw["w2"].shape),
        _const_spec((1, D)), _const_spec(lw["a1"].shape), _const_spec(lw["a2"].shape),
        _const_spec(lw["g1"].shape), _const_spec(lw["g2"].shape),
    ]
    ts, tg = tm // d, d // SUBLANES
    lane = pl.BlockSpec((ts, tg, HEAD, LANES), lambda i: (i, 0, 0, 0))
    lane_shape = jax.ShapeDtypeStruct((T // d, tg, HEAD, LANES), F32)
    out_specs = [lane] * 5 + [tok, pl.BlockSpec((P, D), lambda i: (0, 0))]
    out_shape = [lane_shape] * 5 + [jax.ShapeDtypeStruct((T, D), F32), jax.ShapeDtypeStruct((P, D), F32)]
    return pl.pallas_call(
        functools.partial(_mix_a_body, tm=tm, P=P, d=d),
        grid=(n_tiles,), in_specs=in_specs, out_specs=out_specs, out_shape=out_shape,
        compiler_params=pltpu.CompilerParams(dimension_semantics=("arbitrary",), vmem_limit_bytes=VMEM_LIMIT),
        name="mix_a",
    )(h, h, init_xn, lw["mix_pre_g"], lw["w_in"], lw["w_k"], lw["w_in"], lw["w_in"], lw["mu_rkv"], lw["mu_wag"],
      lw["w0"], lw["w1"], lw["w2"], lw["a0"], lw["a1"], lw["a2"], lw["g1"], lw["g2"])


def _mix_b_tokens(h, gpre, wu, wvg, wgb, vng, vnb):
    xb = _rms(h, gpre).astype(BF16)
    gv = _gelu(_dot(xb, wvg))
    mean = jnp.mean(gv, axis=-1, keepdims=True)
    cen = gv - mean
    var = jnp.mean(cen * cen, axis=-1, keepdims=True)
    u = _gelu(_dot(xb, wu))
    vn = cen * lax.rsqrt(var + EPS) * vng + vnb
    su = jax.nn.sigmoid(_dot(xb, wgb)) * u
    return su, vn


def _mix_b_prompt_body(h_ref, gpre_ref, wu_ref, wvg_ref, wgb_ref, vng_ref, vnb_ref, ws_ref, bias_ref,
                       ob_ref, vn_scr, su_scr, cm_scr, *, tm, nb):
    s = pl.program_id(1)
    n_sub = pl.num_programs(1)
    su, vn = _mix_b_tokens(h_ref[...], gpre_ref[...], wu_ref[...], wvg_ref[...], wgb_ref[...],
                           vng_ref[...], vnb_ref[...])
    row0 = pl.multiple_of(s * tm, tm)
    for g in range(N_GROUPS_B):
        vn_scr[g, pl.ds(row0, tm), :] = vn[:, g * CHUNK:(g + 1) * CHUNK]
        su_scr[g, pl.ds(row0, tm), :] = su[:, g * CHUNK:(g + 1) * CHUNK]

    @pl.when(s == n_sub - 1)
    def _():
        row = lax.broadcasted_iota(jnp.int32, (CHUNK, CHUNK), 0)
        col = lax.broadcasted_iota(jnp.int32, (CHUNK, CHUNK), 1)
        for g in range(N_GROUPS_B):
            wm = jnp.where(row >= col, ws_ref[g], 0.0).astype(BF16)
            bias = bias_ref[:, g * CHUNK:(g + 1) * CHUNK]
            for b in range(nb):
                rows = pl.ds(b, CHUNK, stride=nb)
                mixed = _dot(wm, vn_scr[g, rows, :].astype(BF16)) + bias
                cm_scr[g, rows, :] = mixed * su_scr[g, rows, :]
        for g in range(N_GROUPS_B):
            ob_ref[:, g * CHUNK:(g + 1) * CHUNK] = cm_scr[g]


def _mix_b_prompt(h, lw, *, nb, tm):
    T, D = h.shape
    rows_per_chunk = CHUNK * nb
    n_chunks = T // rows_per_chunk
    n_sub = rows_per_chunk // tm
    wcol = lambda j: _layer_spec(lw["layer"], D, D, j)
    in_specs = [
        pl.BlockSpec((tm, D), lambda c, s: (c * n_sub + s, 0)),
        _const_spec((1, D)), wcol(3), wcol(4), wcol(6),
        _const_spec((1, D)), _const_spec((1, D)),
        _const_spec((N_GROUPS_B, CHUNK, CHUNK)), _const_spec((CHUNK, D)),
    ]
    slab = pltpu.VMEM((N_GROUPS_B, rows_per_chunk, CHUNK), F32)
    return pl.pallas_call(
        functools.partial(_mix_b_prompt_body, tm=tm, nb=nb),
        grid=(n_chunks, n_sub), in_specs=in_specs,
        out_specs=pl.BlockSpec((rows_per_chunk, D), lambda c, s: (c, 0)),
        out_shape=jax.ShapeDtypeStruct((T, D), F32),
        scratch_shapes=[slab, slab, slab],
        compiler_params=pltpu.CompilerParams(dimension_semantics=("arbitrary", "arbitrary"),
                                             vmem_limit_bytes=VMEM_LIMIT),
        name="mix_b_prompt",
    )(h, lw["mix_pre_g"], lw["w_in"], lw["w_in"], lw["w_in"], lw["vn_g"], lw["vn_b"], lw["w_s"], lw["bias_full"])


def _mix_b_sample_body(ws_ref, bs_ref, h_ref, gpre_ref, wu_ref, wvg_ref, wgb_ref, vng_ref, vnb_ref,
                       ob_ref, vn_ref, *, nb, L):
    su, vn = _mix_b_tokens(h_ref[...], gpre_ref[...], wu_ref[...], wvg_ref[...], wgb_ref[...],
                           vng_ref[...], vnb_ref[...])
    vn_ref[...] = vn
    for t in range(L):
        cols = []
        for g in range(N_GROUPS_B):
            acc = jnp.full((nb, CHUNK), bs_ref[g * L + t], F32)
            for s in range(t + 1):
                acc = acc + ws_ref[(g * L + t) * L + s] * vn[s * nb:(s + 1) * nb, g * CHUNK:(g + 1) * CHUNK]
            cols.append(acc)
        ob_ref[t * nb:(t + 1) * nb, :] = jnp.concatenate(cols, axis=1) * su[t * nb:(t + 1) * nb]


def _mix_b_sample(h, lw, *, nb, L):
    T, D = h.shape
    wcol = lambda j: _layer_spec(lw["layer"], D, D, j)
    smem = pl.BlockSpec(memory_space=pltpu.SMEM)
    in_specs = [smem, smem, _const_spec((T, D)), _const_spec((1, D)), wcol(3), wcol(4), wcol(6),
                _const_spec((1, D)), _const_spec((1, D))]
    full = pl.BlockSpec((T, D), lambda i: (0, 0))
    return pl.pallas_call(
        functools.partial(_mix_b_sample_body, nb=nb, L=L),
        grid=(1,), in_specs=in_specs, out_specs=[full, full],
        out_shape=[jax.ShapeDtypeStruct((T, D), F32)] * 2,
        compiler_params=pltpu.CompilerParams(dimension_semantics=("arbitrary",), vmem_limit_bytes=VMEM_LIMIT),
        name="mix_b_sample",
    )(lw["ws_head"], lw["bs_head"], h, lw["mix_pre_g"], lw["w_in"], lw["w_in"], lw["w_in"], lw["vn_g"], lw["vn_b"])


def _wkv_body(r_ref, k_ref, v_ref, w_ref, a_ref, s0_ref, kk_ref, ka_ref, rk_ref, lg_ref, lb_ref,
              o_ref, st_ref, vec_ref, y_ref, pend_ref, *, TT):
    NA, BB, K2, R = range(N_STAGED)
    HALF = HEAD // 2

    @pl.when(pl.program_id(1) == 0)
    def _():
        st_ref[...] = s0_ref[...]

    def stage(tt, p_prev):
        k = k_ref[tt]
        a = a_ref[tt]
        kk = k * kk_ref[...]
        nrm = jnp.sqrt(jnp.sum(kk * kk, axis=0, keepdims=True))
        kk = kk / jnp.maximum(nrm, 1e-12)
        p = p_prev * w_ref[tt]
        inv_p = 1.0 / p
        vec_ref[tt, NA] = -kk * p_prev
        vec_ref[tt, BB] = kk * a * inv_p
        vec_ref[tt, K2] = k * (1.0 + (a - 1.0) * ka_ref[...]) * inv_p
        vec_ref[tt, R] = r_ref[tt] * p
        return p

    def recur(cur, nxt, sa):
        sa_next = []
        for half in range(2):
            rows = slice(half * HALF, (half + 1) * HALF)
            sa_h = sa[rows]
            v_h = v_ref[cur, rows, :]
            y_h = jnp.zeros((HALF, LANES), F32)
            san_h = jnp.zeros((HALF, LANES), F32)
            for j in range(HEAD):
                jr = slice(j, j + 1)
                sn = st_ref[j, rows, :] + sa_h * vec_ref[cur, BB, jr, :] + v_h * vec_ref[cur, K2, jr, :]
                st_ref[j, rows, :] = sn
                y_h = y_h + sn * vec_ref[cur, R, jr, :]
                san_h = san_h + sn * vec_ref[nxt, NA, jr, :]
            y_ref[cur, rows, :] = y_h
            sa_next.append(san_h)
        return jnp.concatenate(sa_next, axis=0)

    def finish(tt, slot):
        y = y_ref[slot]
        mean = jnp.mean(y, axis=0, keepdims=True)
        cen = y - mean
        var = jnp.mean(cen * cen, axis=0, keepdims=True)
        yn = cen * lax.rsqrt(var + GN_EPS) * lg_ref[...] + lb_ref[...]
        bonus = jnp.sum(vec_ref[slot, R] * vec_ref[slot, K2] * rk_ref[...], axis=0, keepdims=True)
        o_ref[tt] = yn + bonus * v_ref[slot]

    p_end = lax.fori_loop(0, TT, stage, jnp.ones((HEAD, LANES), F32), unroll=min(AUX_UNROLL, TT))
    pend_ref[...] = p_end

    sa0 = jnp.zeros((HEAD, LANES), F32)
    for j in range(HEAD):
        sa0 = sa0 + st_ref[j] * vec_ref[0, NA, j:j + 1, :]

    lax.fori_loop(0, TT, lambda tt, sa: recur(tt, jnp.minimum(tt + 1, TT - 1), sa), sa0)

    for j in range(HEAD):
        st_ref[j] = st_ref[j] * pend_ref[j:j + 1, :]

    def finish_step(tt, carry):
        finish(tt, tt)
        return carry

    lax.fori_loop(0, TT, finish_step, 0, unroll=min(AUX_UNROLL, TT))


def _wkv(r, k, v, w, a, s0, g0, lw, *, L, nb, TT):
    G = nb // SUBLANES
    tile = pl.BlockSpec((TT, None, HEAD, LANES), lambda g, t: (t, g, 0, 0))
    st_in = pl.BlockSpec((None, HEAD, HEAD, LANES), lambda g, t: (g0 + g, 0, 0, 0))
    st = pl.BlockSpec((None, HEAD, HEAD, LANES), lambda g, t: (g, 0, 0, 0))
    par = pl.BlockSpec((HEAD, LANES), lambda g, t: (0, 0))
    return pl.pallas_call(
        functools.partial(_wkv_body, TT=TT),
        grid=(G, L // TT),
        in_specs=[tile] * 5 + [st_in] + [par] * 5,
        out_specs=[tile, st],
        out_shape=[jax.ShapeDtypeStruct((L, G, HEAD, LANES), F32),
                   jax.ShapeDtypeStruct((G, HEAD, HEAD, LANES), F32)],
        scratch_shapes=[pltpu.VMEM((TT, N_STAGED, HEAD, LANES), F32), pltpu.VMEM((TT, HEAD, LANES), F32),
                        pltpu.VMEM((HEAD, LANES), F32)],
        compiler_params=pltpu.CompilerParams(dimension_semantics=("arbitrary", "arbitrary"),
                                             vmem_limit_bytes=VMEM_LIMIT),
        name="wkv",
    )(r, k, v, w, a, s0, lw["kk_l"], lw["ka_l"], lw["rk_l"], lw["lg_l"], lw["lb_l"])


def _post_body(h_ref, o_ref, onext_ref, ga_ref, ob_ref, p_ref, gpost_ref, wout_ref, gfpre_ref, wup_ref,
               wdown_ref, gfpost_ref, wpg_ref, wpe_ref, out_ref, o_scr, *, tm, nb):
    def token_rows(ref):
        return [_from_lanes(ref[s, g]) for s, g, _ in _slabs(tm, nb)]

    @pl.when(pl.program_id(0) == 0)
    def _():
        o_scr[...] = jnp.concatenate(token_rows(o_ref), axis=0)

    mixed = ga_ref[...] * o_scr[...] + ob_ref[...]
    h = h_ref[...] + _rms(_dot(mixed.astype(BF16), wout_ref[...]), gpost_ref[...])
    z = _dot(_rms(h, gfpre_ref[...]).astype(BF16), wup_ref[...])
    z = jnp.square(jnp.maximum(z, 0.0))
    h = h + _rms(_dot(z.astype(BF16), wdown_ref[...]), gfpost_ref[...])
    nxt = token_rows(onext_ref)
    o_scr[...] = jnp.concatenate(nxt, axis=0)
    gate = jax.nn.sigmoid(_dot(h.astype(BF16), wpg_ref[...]))
    out_ref[...] = _after(h + gate * _dot(p_ref[...].astype(BF16), wpe_ref[...]), _exact_zero(nxt))


def _post(h, o, ga, ob, p, lw, *, tm, nb):
    T, D = h.shape
    tok = pl.BlockSpec((tm, D), lambda i: (i, 0))
    n_tiles = T // tm
    lane_shape = (tm // nb, nb // SUBLANES, HEAD, LANES)
    lane = pl.BlockSpec(lane_shape, lambda i: (i, 0, 0, 0))
    lane_next = pl.BlockSpec(lane_shape, lambda i: (jnp.minimum(i + 1, n_tiles - 1), 0, 0, 0))
    in_specs = [tok, lane, lane_next, tok, tok, pl.BlockSpec((tm, p.shape[1]), lambda i: (i, 0)),
                _const_spec((1, D)), _layer_spec(lw["layer"], *lw["w_out"].shape[1:]),
                _const_spec((1, D)), _layer_spec(lw["layer"], *lw["w_up"].shape[1:]),
                _layer_spec(lw["layer"], *lw["w_down"].shape[1:]),
                _const_spec((1, D)), _layer_spec(lw["layer"], *lw["w_pg"].shape[1:]),
                _layer_spec(lw["layer"], *lw["w_pe"].shape[1:])]
    return pl.pallas_call(
        functools.partial(_post_body, tm=tm, nb=nb), grid=(n_tiles,), in_specs=in_specs, out_specs=tok,
        out_shape=jax.ShapeDtypeStruct((T, D), F32),
        scratch_shapes=[pltpu.VMEM((tm, D), F32)],
        compiler_params=pltpu.CompilerParams(dimension_semantics=("arbitrary",), vmem_limit_bytes=VMEM_LIMIT),
        name="post",
    )(h, o, o, ga, ob, p, lw["mix_post_g"], lw["w_out"], lw["ffn_pre_g"], lw["w_up"], lw["w_down"],
      lw["ffn_post_g"], lw["w_pg"], lw["w_pe"])


def _lane_param(x, n_heads):
    xh = jnp.swapaxes(x.reshape(n_heads // 2, 2, HEAD), 0, 1).reshape(n_heads, HEAD)
    return jnp.repeat(xh.T, SUBLANES, axis=1).astype(F32)


def _state_to_lanes(s):
    nb, nh = s.shape[0], s.shape[1]
    s = s.reshape(nb // SUBLANES, SUBLANES, nh // 2, 2, HEAD, HEAD)
    return jnp.transpose(s, (0, 5, 4, 3, 2, 1)).reshape(nb // SUBLANES, HEAD, HEAD, nh * SUBLANES)


def _state_from_lanes(s, nh):
    G = s.shape[0]
    s = jnp.transpose(s.reshape(G, HEAD, HEAD, 2, nh // 2, SUBLANES), (0, 5, 4, 3, 2, 1))
    return s.reshape(G * SUBLANES, nh, HEAD, HEAD)


def _layer(h, p, s0, g0, init_xn, lw, *, L, nb, prompt):
    if prompt:
        tm, P = 256, BF16_ROWS
        r, k, v, w, a, ga, last = _mix_a(h, init_xn, lw, tm=tm, P=P, d=nb)
        ob = _mix_b_prompt(h, lw, nb=nb, tm=256)
        vn = None
        TT = WKV_BLOCK
    else:
        r, k, v, w, a, ga, last = _mix_a(h, init_xn, lw, tm=nb, P=nb, d=nb)
        ob, vn = _mix_b_sample(h, lw, nb=nb, L=L)
        TT = L
    o, sT = _wkv(r, k, v, w, a, s0, g0, lw, L=L, nb=nb, TT=TT)
    h = _post(h, o, ga, ob, p, lw, tm=min(256, h.shape[0]), nb=nb)
    return h, sT, last[-nb:], vn


def kernel(x_prompt, x_sample, state_wkv, state_shift, p_prompt, p_sample, mix_pre_g, mix_post_g, ffn_pre_g, ffn_post_g, w_in, mu_rkv, mu_wag, w0, w1, w2, a0, a1, a2, g1, g2, k_k, k_a, r_k, lnx_g, lnx_b, vn_g, vn_b, w_s, b_s, w_out, w_up, w_down, w_pe, w_pg):
    B, L, D = x_prompt.shape
    Bs, Ls, _ = x_sample.shape
    depth = w_in.shape[0]
    nh = D // HEAD
    row = lambda x: x.reshape(1, -1).astype(F32)
    tmajor = lambda x: jnp.swapaxes(x, 0, 1).reshape(-1, x.shape[-1])

    hp = tmajor(x_prompt)
    hs = tmajor(x_sample)
    zero_state = jnp.zeros((B // SUBLANES, HEAD, HEAD, nh * SUBLANES), F32)
    swap_k = lambda x: jnp.concatenate([x[..., :D], _swap_head_pairs(x[..., D:2 * D]), x[..., 2 * D:]], axis=-1)
    big = dict(w_in=w_in.astype(BF16), w_k=_swap_head_pairs(w_in[..., D:2 * D]).astype(BF16),
               w_out=w_out.astype(BF16), w_up=w_up.astype(BF16),
               w_down=w_down.astype(BF16), w_pe=w_pe.astype(BF16), w_pg=w_pg.astype(BF16))
    s_all = _state_to_lanes(state_wkv.reshape((depth * Bs,) + state_wkv.shape[2:]))
    wkv_p, sh_p, wkv_s, sh_s, cv_s = [], [], [], [], []
    for i in range(depth):
        lw = dict(
            big, layer=i,
            mix_pre_g=row(mix_pre_g[i]), mix_post_g=row(mix_post_g[i]),
            ffn_pre_g=row(ffn_pre_g[i]), ffn_post_g=row(ffn_post_g[i]),
            mu_rkv=swap_k(row(mu_rkv[i])), mu_wag=mu_wag[i],
            w0=row(w0[i]), w1=w1[i].astype(BF16), w2=w2[i].astype(BF16),
            a0=_swap_head_pairs(row(a0[i])), a1=a1[i].astype(BF16), a2=_swap_head_pairs(a2[i]).astype(BF16),
            g1=g1[i].astype(BF16), g2=g2[i].astype(BF16),
            kk_l=_lane_param(k_k[i], nh), ka_l=_lane_param(k_a[i], nh), rk_l=_lane_param(r_k[i].reshape(-1), nh),
            lg_l=_lane_param(lnx_g[i], nh), lb_l=_lane_param(lnx_b[i], nh),
            vn_g=row(vn_g[i]), vn_b=row(vn_b[i]), w_s=w_s[i],
            bias_full=jnp.repeat(b_s[i].T, CHUNK, axis=1),
            ws_head=w_s[i][:, :Ls, :Ls].reshape(-1), bs_head=b_s[i][:, :Ls].reshape(-1),
        )
        hp, S_p, last_p, _ = _layer(hp, tmajor(p_prompt[i]), zero_state, 0, jnp.zeros((BF16_ROWS, D), F32), lw,
                                    L=L, nb=B, prompt=True)
        hs, S_s, last_s, vn_s = _layer(hs, tmajor(p_sample[i]), s_all, i * (Bs // SUBLANES), state_shift[i], lw,
                                       L=Ls, nb=Bs, prompt=False)
        wkv_p.append(S_p)
        sh_p.append(last_p)
        wkv_s.append(S_s)
        sh_s.append(last_s)
        cv_s.append(jnp.swapaxes(vn_s.reshape(Ls, Bs, D), 0, 1))
    yp = jnp.swapaxes(hp.reshape(L, B, D), 0, 1)
    ys = jnp.swapaxes(hs.reshape(Ls, Bs, D), 0, 1)
    new_wkv_p = _state_from_lanes(jnp.concatenate(wkv_p), nh).reshape((depth, B) + state_wkv.shape[2:])
    new_wkv_s = _state_from_lanes(jnp.concatenate(wkv_s), nh).reshape(state_wkv.shape)
    return (yp, ys, new_wkv_p, jnp.stack(sh_p), new_wkv_s, jnp.stack(sh_s), jnp.stack(cv_s))
```

```python
import functools

import numpy as np
import jax
import jax.numpy as jnp
from jax import lax
from jax.experimental import pallas as pl
from jax.experimental.pallas import tpu as pltpu

F32 = jnp.float32
BF16 = jnp.bfloat16

HEAD = 64
CHUNK = 128
N_GROUPS_B = 8
EPS = 1e-6
GN_EPS = 64e-5
SQRT_HALF = float(np.sqrt(0.5))
EXP_NEG_HALF = float(np.exp(-0.5))

LANES = 128
SUBLANES = 8
BF16_ROWS = 16
AUX_UNROLL = 4
N_STAGED = 4
WKV_BLOCK = 64
VMEM_LIMIT = 56 * 1024 * 1024


def _const_spec(shape):
    nd = len(shape)
    return pl.BlockSpec(shape, lambda *_: (0,) * nd, pipeline_mode=pl.Buffered(1))


def _layer_spec(layer, rows, cols, col_block=0):
    return pl.BlockSpec((None, rows, cols), lambda *_: (layer, 0, col_block), pipeline_mode=pl.Buffered(1))


def _rms(x, g):
    ms = jnp.mean(x * x, axis=-1, keepdims=True)
    return x * lax.rsqrt(ms + EPS) * g


def _gelu(x):
    return 0.5 * x * (1.0 + lax.erf(x * SQRT_HALF))


def _dot(a, b):
    return jnp.dot(a, b, preferred_element_type=F32)


def _lane_head_order(n_heads):
    return [2 * blk + par for par in range(2) for blk in range(n_heads // 2)]


def _to_lanes(x):
    heads = _lane_head_order(x.shape[1] // HEAD)
    return jnp.concatenate([x[:, h * HEAD:(h + 1) * HEAD] for h in heads], axis=0).T


def _swap_head_pairs(x):
    shape = x.shape
    return jnp.flip(x.reshape(shape[:-1] + (shape[-1] // LANES, 2, HEAD)), axis=-2).reshape(shape)


def _to_lanes2(x1, x2s):
    low = lax.broadcasted_iota(jnp.int32, (SUBLANES, LANES), 1) < HEAD
    first, second = [], []
    for blk in range(x1.shape[1] // LANES):
        a1 = x1[:, blk * LANES:(blk + 1) * LANES]
        a2 = x2s[:, blk * LANES:(blk + 1) * LANES]
        first.append(jnp.where(low, a1, a2))
        second.append(jnp.where(low, a2, a1))
    zt = jnp.concatenate(first + second, axis=0).T
    top, bottom = zt[:HEAD], zt[HEAD:]
    low = lax.broadcasted_iota(jnp.int32, (HEAD, LANES), 1) < HEAD
    return jnp.where(low, top, bottom), jnp.where(low, bottom, top)


def _from_lanes(y):
    n_heads = LANES // SUBLANES
    z = y.T
    blocks = [None] * n_heads
    for pos, h in enumerate(_lane_head_order(n_heads)):
        blocks[h] = z[pos * SUBLANES:(pos + 1) * SUBLANES, :]
    return jnp.concatenate(blocks, axis=1)


def _exact_zero(tiles):
    acc = jnp.zeros((SUBLANES, LANES), jnp.uint32)
    for t in tiles:
        acc = acc | lax.bitcast_convert_type(t[:SUBLANES, :LANES], jnp.uint32)
    acc = lax.shift_right_logical(lax.shift_right_logical(acc, jnp.uint32(16)), jnp.uint32(16))
    return lax.bitcast_convert_type(acc, F32)


def _after(x, zero):
    rows = BF16_ROWS if x.dtype == BF16 else SUBLANES
    z = jnp.concatenate([zero.astype(x.dtype)] * (rows // SUBLANES), axis=0)
    z = jnp.concatenate([z] * (x.shape[1] // LANES), axis=1)
    return x + jnp.concatenate([z] * (x.shape[0] // rows), axis=0)


def _slabs(tm, nb):
    tg = nb // SUBLANES
    return [(s, g, (s * tg + g) * SUBLANES) for s in range(tm // nb) for g in range(tg)]


def _mix_a_body(h_ref, hp_ref, init_ref, gpre_ref, wr_ref, wk_ref, wv_ref, wga_ref, mu_rkv_ref, mu_wag_ref,
                w0_ref, w1_ref, w2_ref, a0_ref, a1_ref, a2_ref, g1_ref, g2_ref,
                r_ref, k_ref, v_ref, w_ref, a_ref, ga_ref, last_ref, *, tm, P, d):
    D = h_ref.shape[-1]
    slabs = _slabs(tm, d)
    i = pl.program_id(0)
    g = gpre_ref[...]
    xn = _rms(h_ref[...], g)
    prev = jnp.where(i == 0, init_ref[...], _rms(hp_ref[...], g))
    xn_ext = jnp.concatenate([prev, xn], axis=0)
    xs = xn_ext[P - d:P - d + tm]
    dx = xs - xn
    mw = mu_wag_ref[...]
    xw = xn + dx * mw[0:1]
    xa = xn + dx * mw[1:2]
    xg = xn + dx * mw[2:3]
    xb_ext = xn_ext.astype(BF16)
    xb = xb_ext[P:]
    mu = mu_rkv_ref[...]

    def shifted_proj(s, lhs):
        pe = _dot(lhs, (wr_ref, wk_ref, wv_ref)[s][...])
        cur = pe[P:]
        prv = pe[P - d:P - d + tm]
        return cur + (prv - cur) * mu[:, s * D:(s + 1) * D]

    exact_zero, after = _exact_zero, _after

    def put_pair(x1, x2s, out1, out2):
        tiles = []
        for s, g, r0 in slabs:
            t1, t2 = _to_lanes2(x1[r0:r0 + SUBLANES], x2s[r0:r0 + SUBLANES])
            out1[s, g], out2[s, g] = t1, t2
            tiles += [t1, t2]
        return exact_zero(tiles)

    lw1 = jnp.tanh(_dot(xw.astype(BF16), w1_ref[...])).astype(BF16)
    la1 = _dot(xa.astype(BF16), a1_ref[...]).astype(BF16)
    lg1 = jax.nn.sigmoid(_dot(xg.astype(BF16), g1_ref[...])).astype(BF16)
    v = shifted_proj(2, xb_ext)
    r = shifted_proj(0, xb_ext)
    wl = w0_ref[...] + _dot(lw1, w2_ref[...])
    w = jnp.exp(-EXP_NEG_HALF * jax.nn.sigmoid(wl))
    a_s = jax.nn.sigmoid(a0_ref[...] + _dot(la1, a2_ref[...]))
    v_tiles = []
    for s, g, r0 in slabs:
        v_tiles.append(_to_lanes(v[r0:r0 + SUBLANES]))
        v_ref[s, g] = v_tiles[-1]
    k_s = shifted_proj(1, after(xb_ext, exact_zero(v_tiles)))
    z_wa = put_pair(w, a_s, w_ref, a_ref)
    gate = _dot(after(xb, z_wa), wga_ref[...])
    z_rk = put_pair(r, k_s, r_ref, k_ref)
    ga_ref[...] = after(jax.nn.sigmoid(gate) * _dot(lg1, g2_ref[...]), z_rk)
    last_ref[...] = xn[tm - P:]


def _mix_a(h, init_xn, lw, *, tm, P, d):
    T, D = h.shape
    n_tiles = T // tm
    ratio = tm // P
    tok = pl.BlockSpec((tm, D), lambda i: (i, 0))
    in_specs = [
        tok,
        pl.BlockSpec((P, D), lambda i: (jnp.maximum(i * ratio - 1, 0), 0)),
        _const_spec((P, D)),
        _const_spec((1, D)),
        _layer_spec(lw["layer"], D, D, 0),
        _layer_spec(lw["layer"], D, D, 0),
        _layer_spec(lw["layer"], D, D, 2),
        _layer_spec(lw["layer"], D, D, 5),
        _const_spec((1, 3 * D)), _const_spec((3, D)),
        _const_spec((1, D)), _const_spec(lw["w1"].shape), _const_spec(lw["w2"].shape),
        _const_spec((1, D)), _const_spec(lw["a1"].shape), _const_spec(lw["a2"].shape),
        _const_spec(lw["g1"].shape), _const_spec(lw["g2"].shape),
    ]
    ts, tg = tm // d, d // SUBLANES
    lane = pl.BlockSpec((ts, tg, HEAD, LANES), lambda i: (i, 0, 0, 0))
    lane_shape = jax.ShapeDtypeStruct((T // d, tg, HEAD, LANES), F32)
    out_specs = [lane] * 5 + [tok, pl.BlockSpec((P, D), lambda i: (0, 0))]
    out_shape = [lane_shape] * 5 + [jax.ShapeDtypeStruct((T, D), F32), jax.ShapeDtypeStruct((P, D), F32)]
    return pl.pallas_call(
        functools.partial(_mix_a_body, tm=tm, P=P, d=d),
        grid=(n_tiles,), in_specs=in_specs, out_specs=out_specs, out_shape=out_shape,
        compiler_params=pltpu.CompilerParams(dimension_semantics=("arbitrary",), vmem_limit_bytes=VMEM_LIMIT),
        name="mix_a",
    )(h, h, init_xn, lw["mix_pre_g"], lw["w_in"], lw["w_k"], lw["w_in"], lw["w_in"], lw["mu_rkv"], lw["mu_wag"],
      lw["w0"], lw["w1"], lw["w2"], lw["a0"], lw["a1"], lw["a2"], lw["g1"], lw["g2"])


def _mix_b_tokens(h, gpre, wu, wvg, wgb, vng, vnb):
    xb = _rms(h, gpre).astype(BF16)
    gv = _gelu(_dot(xb, wvg))
    mean = jnp.mean(gv, axis=-1, keepdims=True)
    cen = gv - mean
    var = jnp.mean(cen * cen, axis=-1, keepdims=True)
    u = _gelu(_dot(xb, wu))
    vn = cen * lax.rsqrt(var + EPS) * vng + vnb
    su = jax.nn.sigmoid(_dot(xb, wgb)) * u
    return su, vn


def _mix_b_prompt_body(h_ref, gpre_ref, wu_ref, wvg_ref, wgb_ref, vng_ref, vnb_ref, ws_ref, bias_ref,
                       ob_ref, vn_scr, su_scr, cm_scr, *, tm, nb):
    s = pl.program_id(1)
    n_sub = pl.num_programs(1)
    su, vn = _mix_b_tokens(h_ref[...], gpre_ref[...], wu_ref[...], wvg_ref[...], wgb_ref[...],
                           vng_ref[...], vnb_ref[...])
    row0 = pl.multiple_of(s * tm, tm)
    for g in range(N_GROUPS_B):
        vn_scr[g, pl.ds(row0, tm), :] = vn[:, g * CHUNK:(g + 1) * CHUNK]
        su_scr[g, pl.ds(row0, tm), :] = su[:, g * CHUNK:(g + 1) * CHUNK]

    @pl.when(s == n_sub - 1)
    def _():
        row = lax.broadcasted_iota(jnp.int32, (CHUNK, CHUNK), 0)
        col = lax.broadcasted_iota(jnp.int32, (CHUNK, CHUNK), 1)
        for g in range(N_GROUPS_B):
            wm = jnp.where(row >= col, ws_ref[g], 0.0).astype(BF16)
            bias = bias_ref[:, g * CHUNK:(g + 1) * CHUNK]
            seq_rows = [pl.ds(b, CHUNK, stride=nb) for b in range(nb)]
            vb = jnp.concatenate([vn_scr[g, rows, :].astype(BF16) for rows in seq_rows], axis=1)
            mixed = _dot(wm, vb)
            for b, rows in enumerate(seq_rows):
                cm_scr[g, rows, :] = (mixed[:, b * CHUNK:(b + 1) * CHUNK] + bias) * su_scr[g, rows, :]
        for g in range(N_GROUPS_B):
            ob_ref[:, g * CHUNK:(g + 1) * CHUNK] = cm_scr[g]


def _mix_b_prompt(h, lw, *, nb, tm):
    T, D = h.shape
    rows_per_chunk = CHUNK * nb
    n_chunks = T // rows_per_chunk
    n_sub = rows_per_chunk // tm
    wcol = lambda j: _layer_spec(lw["layer"], D, D, j)
    in_specs = [
        pl.BlockSpec((tm, D), lambda c, s: (c * n_sub + s, 0)),
        _const_spec((1, D)), wcol(3), wcol(4), wcol(6),
        _const_spec((1, D)), _const_spec((1, D)),
        _const_spec((N_GROUPS_B, CHUNK, CHUNK)), _const_spec((CHUNK, D)),
    ]
    slab = pltpu.VMEM((N_GROUPS_B, rows_per_chunk, CHUNK), F32)
    return pl.pallas_call(
        functools.partial(_mix_b_prompt_body, tm=tm, nb=nb),
        grid=(n_chunks, n_sub), in_specs=in_specs,
        out_specs=pl.BlockSpec((rows_per_chunk, D), lambda c, s: (c, 0)),
        out_shape=jax.ShapeDtypeStruct((T, D), F32),
        scratch_shapes=[slab, slab, slab],
        compiler_params=pltpu.CompilerParams(dimension_semantics=("arbitrary", "arbitrary"),
                                             vmem_limit_bytes=VMEM_LIMIT),
        name="mix_b_prompt",
    )(h, lw["mix_pre_g"], lw["w_in"], lw["w_in"], lw["w_in"], lw["vn_g"], lw["vn_b"], lw["w_s"], lw["bias_full"])


def _mix_b_sample_body(ws_ref, bs_ref, h_ref, gpre_ref, wu_ref, wvg_ref, wgb_ref, vng_ref, vnb_ref,
                       ob_ref, vn_ref, *, nb, L):
    su, vn = _mix_b_tokens(h_ref[...], gpre_ref[...], wu_ref[...], wvg_ref[...], wgb_ref[...],
                           vng_ref[...], vnb_ref[...])
    vn_ref[...] = vn
    for t in range(L):
        cols = []
        for g in range(N_GROUPS_B):
            acc = jnp.full((nb, CHUNK), bs_ref[g * L + t], F32)
            for s in range(t + 1):
                acc = acc + ws_ref[(g * L + t) * L + s] * vn[s * nb:(s + 1) * nb, g * CHUNK:(g + 1) * CHUNK]
            cols.append(acc)
        ob_ref[t * nb:(t + 1) * nb, :] = jnp.concatenate(cols, axis=1) * su[t * nb:(t + 1) * nb]


def _mix_b_sample(h, lw, *, nb, L):
    T, D = h.shape
    wcol = lambda j: _layer_spec(lw["layer"], D, D, j)
    smem = pl.BlockSpec(memory_space=pltpu.SMEM)
    in_specs = [smem, smem, _const_spec((T, D)), _const_spec((1, D)), wcol(3), wcol(4), wcol(6),
                _const_spec((1, D)), _const_spec((1, D))]
    full = pl.BlockSpec((T, D), lambda i: (0, 0))
    return pl.pallas_call(
        functools.partial(_mix_b_sample_body, nb=nb, L=L),
        grid=(1,), in_specs=in_specs, out_specs=[full, full],
        out_shape=[jax.ShapeDtypeStruct((T, D), F32)] * 2,
        compiler_params=pltpu.CompilerParams(dimension_semantics=("arbitrary",), vmem_limit_bytes=VMEM_LIMIT),
        name="mix_b_sample",
    )(lw["ws_head"], lw["bs_head"], h, lw["mix_pre_g"], lw["w_in"], lw["w_in"], lw["w_in"], lw["vn_g"], lw["vn_b"])


def _wkv_body(r_ref, k_ref, v_ref, w_ref, a_ref, s0_ref, kk_ref, ka_ref, rk_ref, lg_ref, lb_ref,
              o_ref, st_ref, vec_ref, y_ref, pend_ref, *, TT):
    NA, BB, K2, R = range(N_STAGED)
    HALF = HEAD // 2

    @pl.when(pl.program_id(1) == 0)
    def _():
        st_ref[...] = s0_ref[...]

    def stage(tt, p_prev):
        k = k_ref[tt]
        a = a_ref[tt]
        kk = k * kk_ref[...]
        nrm = jnp.sqrt(jnp.sum(kk * kk, axis=0, keepdims=True))
        kk = kk / jnp.maximum(nrm, 1e-12)
        p = p_prev * w_ref[tt]
        inv_p = 1.0 / p
        vec_ref[tt, NA] = -kk * p_prev
        vec_ref[tt, BB] = kk * a * inv_p
        vec_ref[tt, K2] = k * (1.0 + (a - 1.0) * ka_ref[...]) * inv_p
        vec_ref[tt, R] = r_ref[tt] * p
        return p

    def recur(cur, nxt, sa):
        sa_next = []
        for half in range(2):
            rows = slice(half * HALF, (half + 1) * HALF)
            sa_h = sa[rows]
            v_h = v_ref[cur, rows, :]
            y_h = jnp.zeros((HALF, LANES), F32)
            san_h = jnp.zeros((HALF, LANES), F32)
            for j in range(HEAD):
                jr = slice(j, j + 1)
                sn = st_ref[j, rows, :] + sa_h * vec_ref[cur, BB, jr, :] + v_h * vec_ref[cur, K2, jr, :]
                st_ref[j, rows, :] = sn
                y_h = y_h + sn * vec_ref[cur, R, jr, :]
                san_h = san_h + sn * vec_ref[nxt, NA, jr, :]
            y_ref[cur, rows, :] = y_h
            sa_next.append(san_h)
        return jnp.concatenate(sa_next, axis=0)

    def finish(tt, slot):
        y = y_ref[slot]
        mean = jnp.mean(y, axis=0, keepdims=True)
        cen = y - mean
        var = jnp.mean(cen * cen, axis=0, keepdims=True)
        yn = cen * lax.rsqrt(var + GN_EPS) * lg_ref[...] + lb_ref[...]
        bonus = jnp.sum(vec_ref[slot, R] * vec_ref[slot, K2] * rk_ref[...], axis=0, keepdims=True)
        o_ref[tt] = yn + bonus * v_ref[slot]

    p_end = lax.fori_loop(0, TT, stage, jnp.ones((HEAD, LANES), F32), unroll=min(AUX_UNROLL, TT))
    pend_ref[...] = p_end

    sa0 = jnp.zeros((HEAD, LANES), F32)
    for j in range(HEAD):
        sa0 = sa0 + st_ref[j] * vec_ref[0, NA, j:j + 1, :]

    lax.fori_loop(0, TT, lambda tt, sa: recur(tt, jnp.minimum(tt + 1, TT - 1), sa), sa0)

    for j in range(HEAD):
        st_ref[j] = st_ref[j] * pend_ref[j:j + 1, :]

    def finish_step(tt, carry):
        finish(tt, tt)
        return carry

    lax.fori_loop(0, TT, finish_step, 0, unroll=min(AUX_UNROLL, TT))


def _wkv(r, k, v, w, a, s0, g0, lw, *, L, nb, TT):
    G = nb // SUBLANES
    tile = pl.BlockSpec((TT, None, HEAD, LANES), lambda g, t: (t, g, 0, 0))
    st_in = pl.BlockSpec((None, HEAD, HEAD, LANES), lambda g, t: (g0 + g, 0, 0, 0))
    st = pl.BlockSpec((None, HEAD, HEAD, LANES), lambda g, t: (g, 0, 0, 0))
    par = pl.BlockSpec((HEAD, LANES), lambda g, t: (0, 0))
    return pl.pallas_call(
        functools.partial(_wkv_body, TT=TT),
        grid=(G, L // TT),
        in_specs=[tile] * 5 + [st_in] + [par] * 5,
        out_specs=[tile, st],
        out_shape=[jax.ShapeDtypeStruct((L, G, HEAD, LANES), F32),
                   jax.ShapeDtypeStruct((G, HEAD, HEAD, LANES), F32)],
        scratch_shapes=[pltpu.VMEM((TT, N_STAGED, HEAD, LANES), F32), pltpu.VMEM((TT, HEAD, LANES), F32),
                        pltpu.VMEM((HEAD, LANES), F32)],
        compiler_params=pltpu.CompilerParams(dimension_semantics=("arbitrary", "arbitrary"),
                                             vmem_limit_bytes=VMEM_LIMIT),
        name="wkv",
    )(r, k, v, w, a, s0, lw["kk_l"], lw["ka_l"], lw["rk_l"], lw["lg_l"], lw["lb_l"])


def _post_body(h_ref, o_ref, onext_ref, ga_ref, ob_ref, p_ref, gpost_ref, wout_ref, gfpre_ref, wup_ref,
               wdown_ref, gfpost_ref, wpg_ref, wpe_ref, out_ref, o_scr, *, tm, nb):
    def token_rows(ref):
        return [_from_lanes(ref[s, g]) for s, g, _ in _slabs(tm, nb)]

    @pl.when(pl.program_id(0) == 0)
    def _():
        o_scr[...] = jnp.concatenate(token_rows(o_ref), axis=0)

    mixed = ga_ref[...] * o_scr[...] + ob_ref[...]
    h = h_ref[...] + _rms(_dot(mixed.astype(BF16), wout_ref[...]), gpost_ref[...])
    z = _dot(_rms(h, gfpre_ref[...]).astype(BF16), wup_ref[...])
    z = jnp.square(jnp.maximum(z, 0.0))
    h = h + _rms(_dot(z.astype(BF16), wdown_ref[...]), gfpost_ref[...])
    nxt = token_rows(onext_ref)
    o_scr[...] = jnp.concatenate(nxt, axis=0)
    gate = jax.nn.sigmoid(_dot(h.astype(BF16), wpg_ref[...]))
    out_ref[...] = _after(h + gate * _dot(p_ref[...].astype(BF16), wpe_ref[...]), _exact_zero(nxt))


def _post(h, o, ga, ob, p, lw, *, tm, nb):
    T, D = h.shape
    tok = pl.BlockSpec((tm, D), lambda i: (i, 0))
    n_tiles = T // tm
    lane_shape = (tm // nb, nb // SUBLANES, HEAD, LANES)
    lane = pl.BlockSpec(lane_shape, lambda i: (i, 0, 0, 0))
    lane_next = pl.BlockSpec(lane_shape, lambda i: (jnp.minimum(i + 1, n_tiles - 1), 0, 0, 0))
    in_specs = [tok, lane, lane_next, tok, tok, pl.BlockSpec((tm, p.shape[1]), lambda i: (i, 0)),
                _const_spec((1, D)), _layer_spec(lw["layer"], *lw["w_out"].shape[1:]),
                _const_spec((1, D)), _layer_spec(lw["layer"], *lw["w_up"].shape[1:]),
                _layer_spec(lw["layer"], *lw["w_down"].shape[1:]),
                _const_spec((1, D)), _layer_spec(lw["layer"], *lw["w_pg"].shape[1:]),
                _layer_spec(lw["layer"], *lw["w_pe"].shape[1:])]
    return pl.pallas_call(
        functools.partial(_post_body, tm=tm, nb=nb), grid=(n_tiles,), in_specs=in_specs, out_specs=tok,
        out_shape=jax.ShapeDtypeStruct((T, D), F32),
        scratch_shapes=[pltpu.VMEM((tm, D), F32)],
        compiler_params=pltpu.CompilerParams(dimension_semantics=("arbitrary",), vmem_limit_bytes=VMEM_LIMIT),
        name="post",
    )(h, o, o, ga, ob, p, lw["mix_post_g"], lw["w_out"], lw["ffn_pre_g"], lw["w_up"], lw["w_down"],
      lw["ffn_post_g"], lw["w_pg"], lw["w_pe"])


def _lane_param(x, n_heads):
    xh = jnp.swapaxes(x.reshape(n_heads // 2, 2, HEAD), 0, 1).reshape(n_heads, HEAD)
    return jnp.repeat(xh.T, SUBLANES, axis=1).astype(F32)


def _state_to_lanes(s):
    nb, nh = s.shape[0], s.shape[1]
    s = s.reshape(nb // SUBLANES, SUBLANES, nh // 2, 2, HEAD, HEAD)
    return jnp.transpose(s, (0, 5, 4, 3, 2, 1)).reshape(nb // SUBLANES, HEAD, HEAD, nh * SUBLANES)


def _state_from_lanes(s, nh):
    G = s.shape[0]
    s = jnp.transpose(s.reshape(G, HEAD, HEAD, 2, nh // 2, SUBLANES), (0, 5, 4, 3, 2, 1))
    return s.reshape(G * SUBLANES, nh, HEAD, HEAD)


def _layer(h, p, s0, g0, init_xn, lw, *, L, nb, prompt):
    if prompt:
        tm, P = 256, BF16_ROWS
        r, k, v, w, a, ga, last = _mix_a(h, init_xn, lw, tm=tm, P=P, d=nb)
        ob = _mix_b_prompt(h, lw, nb=nb, tm=256)
        vn = None
        TT = WKV_BLOCK
    else:
        r, k, v, w, a, ga, last = _mix_a(h, init_xn, lw, tm=nb, P=nb, d=nb)
        ob, vn = _mix_b_sample(h, lw, nb=nb, L=L)
        TT = L
    o, sT = _wkv(r, k, v, w, a, s0, g0, lw, L=L, nb=nb, TT=TT)
    h = _post(h, o, ga, ob, p, lw, tm=min(256, h.shape[0]), nb=nb)
    return h, sT, last[-nb:], vn


def kernel(x_prompt, x_sample, state_wkv, state_shift, p_prompt, p_sample, mix_pre_g, mix_post_g, ffn_pre_g, ffn_post_g, w_in, mu_rkv, mu_wag, w0, w1, w2, a0, a1, a2, g1, g2, k_k, k_a, r_k, lnx_g, lnx_b, vn_g, vn_b, w_s, b_s, w_out, w_up, w_down, w_pe, w_pg):
    B, L, D = x_prompt.shape
    Bs, Ls, _ = x_sample.shape
    depth = w_in.shape[0]
    nh = D // HEAD
    row = lambda x: x.reshape(1, -1).astype(F32)
    tmajor = lambda x: jnp.swapaxes(x, 0, 1).reshape(-1, x.shape[-1])

    hp = tmajor(x_prompt)
    hs = tmajor(x_sample)
    zero_state = jnp.zeros((B // SUBLANES, HEAD, HEAD, nh * SUBLANES), F32)
    swap_k = lambda x: jnp.concatenate([x[..., :D], _swap_head_pairs(x[..., D:2 * D]), x[..., 2 * D:]], axis=-1)
    big = dict(w_in=w_in.astype(BF16), w_k=_swap_head_pairs(w_in[..., D:2 * D]).astype(BF16),
               w_out=w_out.astype(BF16), w_up=w_up.astype(BF16),
               w_down=w_down.astype(BF16), w_pe=w_pe.astype(BF16), w_pg=w_pg.astype(BF16))
    s_all = _state_to_lanes(state_wkv.reshape((depth * Bs,) + state_wkv.shape[2:]))
    wkv_p, sh_p, wkv_s, sh_s, cv_s = [], [], [], [], []
    for i in range(depth):
        lw = dict(
            big, layer=i,
            mix_pre_g=row(mix_pre_g[i]), mix_post_g=row(mix_post_g[i]),
            ffn_pre_g=row(ffn_pre_g[i]), ffn_post_g=row(ffn_post_g[i]),
            mu_rkv=swap_k(row(mu_rkv[i])), mu_wag=mu_wag[i],
            w0=row(w0[i]), w1=w1[i].astype(BF16), w2=w2[i].astype(BF16),
            a0=_swap_head_pairs(row(a0[i])), a1=a1[i].astype(BF16), a2=_swap_head_pairs(a2[i]).astype(BF16),
            g1=g1[i].astype(BF16), g2=g2[i].astype(BF16),
            kk_l=_lane_param(k_k[i], nh), ka_l=_lane_param(k_a[i], nh), rk_l=_lane_param(r_k[i].reshape(-1), nh),
            lg_l=_lane_param(lnx_g[i], nh), lb_l=_lane_param(lnx_b[i], nh),
            vn_g=row(vn_g[i]), vn_b=row(vn_b[i]), w_s=w_s[i],
            bias_full=jnp.repeat(b_s[i].T, CHUNK, axis=1),
            ws_head=w_s[i][:, :Ls, :Ls].reshape(-1), bs_head=b_s[i][:, :Ls].reshape(-1),
        )
        hp, S_p, last_p, _ = _layer(hp, tmajor(p_prompt[i]), zero_state, 0, jnp.zeros((BF16_ROWS, D), F32), lw,
                                    L=L, nb=B, prompt=True)
        hs, S_s, last_s, vn_s = _layer(hs, tmajor(p_sample[i]), s_all, i * (Bs // SUBLANES), state_shift[i], lw,
                                       L=Ls, nb=Bs, prompt=False)
        wkv_p.append(S_p)
        sh_p.append(last_p)
        wkv_s.append(S_s)
        sh_s.append(last_s)
        cv_s.append(jnp.swapaxes(vn_s.reshape(Ls, Bs, D), 0, 1))
    yp = jnp.swapaxes(hp.reshape(L, B, D), 0, 1)
    ys = jnp.swapaxes(hs.reshape(Ls, Bs, D), 0, 1)
    new_wkv_p = _state_from_lanes(jnp.concatenate(wkv_p), nh).reshape((depth, B) + state_wkv.shape[2:])
    new_wkv_s = _state_from_lanes(jnp.concatenate(wkv_s), nh).reshape(state_wkv.shape)
    return (yp, ys, new_wkv_p, jnp.stack(sh_p), new_wkv_s, jnp.stack(sh_s), jnp.stack(cv_s))
```

```python
import functools

import numpy as np
import jax
import jax.numpy as jnp
from jax import lax
from jax.experimental import pallas as pl
from jax.experimental.pallas import tpu as pltpu

F32 = jnp.float32
BF16 = jnp.bfloat16

HEAD = 64
CHUNK = 128
N_GROUPS_B = 8
EPS = 1e-6
GN_EPS = 64e-5
SQRT_HALF = float(np.sqrt(0.5))
EXP_NEG_HALF = float(np.exp(-0.5))

LANES = 128
SUBLANES = 8
BF16_ROWS = 16
AUX_UNROLL = 4
N_STAGED = 4
WKV_BLOCK = 64
VMEM_LIMIT = 56 * 1024 * 1024


def _const_spec(shape):
    nd = len(shape)
    return pl.BlockSpec(shape, lambda *_: (0,) * nd, pipeline_mode=pl.Buffered(1))


def _layer_spec(layer, rows, cols, col_block=0):
    return pl.BlockSpec((None, rows, cols), lambda *_: (layer, 0, col_block), pipeline_mode=pl.Buffered(1))


def _rms(x, g):
    ms = jnp.mean(x * x, axis=-1, keepdims=True)
    return x * lax.rsqrt(ms + EPS) * g


def _gelu(x):
    return 0.5 * x * (1.0 + lax.erf(x * SQRT_HALF))


def _dot(a, b):
    return jnp.dot(a, b, preferred_element_type=F32)


def _lane_head_order(n_heads):
    return [2 * blk + par for par in range(2) for blk in range(n_heads // 2)]


def _to_lanes(x):
    heads = _lane_head_order(x.shape[1] // HEAD)
    return jnp.concatenate([x[:, h * HEAD:(h + 1) * HEAD] for h in heads], axis=0).T


def _swap_head_pairs(x):
    shape = x.shape
    return jnp.flip(x.reshape(shape[:-1] + (shape[-1] // LANES, 2, HEAD)), axis=-2).reshape(shape)


def _to_lanes2(x1, x2s):
    low = lax.broadcasted_iota(jnp.int32, (SUBLANES, LANES), 1) < HEAD
    first, second = [], []
    for blk in range(x1.shape[1] // LANES):
        a1 = x1[:, blk * LANES:(blk + 1) * LANES]
        a2 = x2s[:, blk * LANES:(blk + 1) * LANES]
        first.append(jnp.where(low, a1, a2))
        second.append(jnp.where(low, a2, a1))
    zt = jnp.concatenate(first + second, axis=0).T
    top, bottom = zt[:HEAD], zt[HEAD:]
    low = lax.broadcasted_iota(jnp.int32, (HEAD, LANES), 1) < HEAD
    return jnp.where(low, top, bottom), jnp.where(low, bottom, top)


def _from_lanes(y):
    n_heads = LANES // SUBLANES
    z = y.T
    blocks = [None] * n_heads
    for pos, h in enumerate(_lane_head_order(n_heads)):
        blocks[h] = z[pos * SUBLANES:(pos + 1) * SUBLANES, :]
    return jnp.concatenate(blocks, axis=1)


def _exact_zero(tiles):
    acc = jnp.zeros((SUBLANES, LANES), jnp.uint32)
    for t in tiles:
        acc = acc | lax.bitcast_convert_type(t[:SUBLANES, :LANES], jnp.uint32)
    acc = lax.shift_right_logical(lax.shift_right_logical(acc, jnp.uint32(16)), jnp.uint32(16))
    return lax.bitcast_convert_type(acc, F32)


def _after(x, zero):
    rows = BF16_ROWS if x.dtype == BF16 else SUBLANES
    z = jnp.concatenate([zero.astype(x.dtype)] * (rows // SUBLANES), axis=0)
    z = jnp.concatenate([z] * (x.shape[1] // LANES), axis=1)
    return x + jnp.concatenate([z] * (x.shape[0] // rows), axis=0)


def _slabs(tm, nb):
    tg = nb // SUBLANES
    return [(s, g, (s * tg + g) * SUBLANES) for s in range(tm // nb) for g in range(tg)]


def _mix_a_body(h_ref, hp_ref, init_ref, gpre_ref, wr_ref, wk_ref, wv_ref, wga_ref, mu_rkv_ref, mu_wag_ref,
                w0_ref, w1_ref, w2_ref, a0_ref, a1_ref, a2_ref, g1_ref, g2_ref,
                r_ref, k_ref, v_ref, w_ref, a_ref, ga_ref, last_ref, wbf_ref, *, tm, P, d):
    D = h_ref.shape[-1]
    slabs = _slabs(tm, d)
    i = pl.program_id(0)

    @pl.when(i == 0)
    def _():
        wbf_ref[0] = wr_ref[...].astype(BF16)
        wbf_ref[1] = wv_ref[...].astype(BF16)
        wbf_ref[2] = wga_ref[...].astype(BF16)
    g = gpre_ref[...]
    xn = _rms(h_ref[...], g)
    prev = jnp.where(i == 0, init_ref[...], _rms(hp_ref[...], g))
    xn_ext = jnp.concatenate([prev, xn], axis=0)
    xs = xn_ext[P - d:P - d + tm]
    dx = xs - xn
    mw = mu_wag_ref[...]
    xw = xn + dx * mw[0:1]
    xa = xn + dx * mw[1:2]
    xg = xn + dx * mw[2:3]
    xb_ext = xn_ext.astype(BF16)
    xb = xb_ext[P:]
    mu = mu_rkv_ref[...]

    def shifted_proj(s, lhs):
        pe = _dot(lhs, (wbf_ref[0], wk_ref[...], wbf_ref[1])[s])
        cur = pe[P:]
        prv = pe[P - d:P - d + tm]
        return cur + (prv - cur) * mu[:, s * D:(s + 1) * D]

    exact_zero, after = _exact_zero, _after

    def put_pair(x1, x2s, out1, out2):
        tiles = []
        for s, g, r0 in slabs:
            t1, t2 = _to_lanes2(x1[r0:r0 + SUBLANES], x2s[r0:r0 + SUBLANES])
            out1[s, g], out2[s, g] = t1, t2
            tiles += [t1, t2]
        return exact_zero(tiles)

    lw1 = jnp.tanh(_dot(xw.astype(BF16), w1_ref[...])).astype(BF16)
    la1 = _dot(xa.astype(BF16), a1_ref[...]).astype(BF16)
    lg1 = jax.nn.sigmoid(_dot(xg.astype(BF16), g1_ref[...])).astype(BF16)
    v = shifted_proj(2, xb_ext)
    r = shifted_proj(0, xb_ext)
    wl = w0_ref[...] + _dot(lw1, w2_ref[...])
    w = jnp.exp(-EXP_NEG_HALF * jax.nn.sigmoid(wl))
    a_s = jax.nn.sigmoid(a0_ref[...] + _dot(la1, a2_ref[...]))
    v_tiles = []
    for s, g, r0 in slabs:
        v_tiles.append(_to_lanes(v[r0:r0 + SUBLANES]))
        v_ref[s, g] = v_tiles[-1]
    k_s = shifted_proj(1, after(xb_ext, exact_zero(v_tiles)))
    z_wa = put_pair(w, a_s, w_ref, a_ref)
    gate = _dot(after(xb, z_wa), wbf_ref[2])
    z_rk = put_pair(r, k_s, r_ref, k_ref)
    ga_ref[...] = after(jax.nn.sigmoid(gate) * _dot(lg1, g2_ref[...]), z_rk)
    last_ref[...] = xn[tm - P:]


def _mix_a(h, init_xn, lw, *, tm, P, d):
    T, D = h.shape
    n_tiles = T // tm
    ratio = tm // P
    tok = pl.BlockSpec((tm, D), lambda i: (i, 0))
    in_specs = [
        tok,
        pl.BlockSpec((P, D), lambda i: (jnp.maximum(i * ratio - 1, 0), 0)),
        _const_spec((P, D)),
        _const_spec((1, D)),
        _layer_spec(lw["layer"], D, D, 0),
        _layer_spec(lw["layer"], D, D, 0),
        _layer_spec(lw["layer"], D, D, 2),
        _layer_spec(lw["layer"], D, D, 5),
        _const_spec((1, 3 * D)), _const_spec((3, D)),
        _const_spec((1, D)), _const_spec(lw["w1"].shape), _const_spec(lw["w2"].shape),
        _const_spec((1, D)), _const_spec(lw["a1"].shape), _const_spec(lw["a2"].shape),
        _const_spec(lw["g1"].shape), _const_spec(lw["g2"].shape),
    ]
    ts, tg = tm // d, d // SUBLANES
    lane = pl.BlockSpec((ts, tg, HEAD, LANES), lambda i: (i, 0, 0, 0))
    lane_shape = jax.ShapeDtypeStruct((T // d, tg, HEAD, LANES), F32)
    out_specs = [lane] * 5 + [tok, pl.BlockSpec((P, D), lambda i: (0, 0))]
    out_shape = [lane_shape] * 5 + [jax.ShapeDtypeStruct((T, D), F32), jax.ShapeDtypeStruct((P, D), F32)]
    return pl.pallas_call(
        functools.partial(_mix_a_body, tm=tm, P=P, d=d),
        grid=(n_tiles,), in_specs=in_specs, out_specs=out_specs, out_shape=out_shape,
        scratch_shapes=[pltpu.VMEM((3, D, D), BF16)],
        compiler_params=pltpu.CompilerParams(dimension_semantics=("arbitrary",), vmem_limit_bytes=VMEM_LIMIT),
        name="mix_a",
    )(h, h, init_xn, lw["mix_pre_g"], lw["w_in"], lw["w_k"], lw["w_in"], lw["w_in"], lw["mu_rkv"], lw["mu_wag"],
      lw["w0"], lw["w1"], lw["w2"], lw["a0"], lw["a1"], lw["a2"], lw["g1"], lw["g2"])


def _mix_b_tokens(h, gpre, wu, wvg, wgb, vng, vnb):
    xb = _rms(h, gpre).astype(BF16)
    gv = _gelu(_dot(xb, wvg))
    mean = jnp.mean(gv, axis=-1, keepdims=True)
    cen = gv - mean
    var = jnp.mean(cen * cen, axis=-1, keepdims=True)
    u = _gelu(_dot(xb, wu))
    vn = cen * lax.rsqrt(var + EPS) * vng + vnb
    su = jax.nn.sigmoid(_dot(xb, wgb)) * u
    return su, vn


def _mix_b_prompt_body(h_ref, gpre_ref, wu_ref, wvg_ref, wgb_ref, vng_ref, vnb_ref, ws_ref, bias_ref,
                       ob_ref, vn_scr, su_scr, cm_scr, wbf_ref, *, tm, nb):
    s = pl.program_id(1)
    n_sub = pl.num_programs(1)

    @pl.when((pl.program_id(0) == 0) & (s == 0))
    def _():
        wbf_ref[0] = wu_ref[...].astype(BF16)
        wbf_ref[1] = wvg_ref[...].astype(BF16)
        wbf_ref[2] = wgb_ref[...].astype(BF16)

    su, vn = _mix_b_tokens(h_ref[...], gpre_ref[...], wbf_ref[0], wbf_ref[1], wbf_ref[2],
                           vng_ref[...], vnb_ref[...])
    row0 = pl.multiple_of(s * tm, tm)
    for g in range(N_GROUPS_B):
        vn_scr[g, pl.ds(row0, tm), :] = vn[:, g * CHUNK:(g + 1) * CHUNK]
        su_scr[g, pl.ds(row0, tm), :] = su[:, g * CHUNK:(g + 1) * CHUNK]

    @pl.when(s == n_sub - 1)
    def _():
        row = lax.broadcasted_iota(jnp.int32, (CHUNK, CHUNK), 0)
        col = lax.broadcasted_iota(jnp.int32, (CHUNK, CHUNK), 1)
        for g in range(N_GROUPS_B):
            wm = jnp.where(row >= col, ws_ref[g], 0.0).astype(BF16)
            bias = bias_ref[:, g * CHUNK:(g + 1) * CHUNK]
            seq_rows = [pl.ds(b, CHUNK, stride=nb) for b in range(nb)]
            vb = jnp.concatenate([vn_scr[g, rows, :].astype(BF16) for rows in seq_rows], axis=1)
            mixed = _dot(wm, vb)
            for b, rows in enumerate(seq_rows):
                cm_scr[g, rows, :] = (mixed[:, b * CHUNK:(b + 1) * CHUNK] + bias) * su_scr[g, rows, :]
        for g in range(N_GROUPS_B):
            ob_ref[:, g * CHUNK:(g + 1) * CHUNK] = cm_scr[g]


def _mix_b_prompt(h, lw, *, nb, tm):
    T, D = h.shape
    rows_per_chunk = CHUNK * nb
    n_chunks = T // rows_per_chunk
    n_sub = rows_per_chunk // tm
    wcol = lambda j: _layer_spec(lw["layer"], D, D, j)
    in_specs = [
        pl.BlockSpec((tm, D), lambda c, s: (c * n_sub + s, 0)),
        _const_spec((1, D)), wcol(3), wcol(4), wcol(6),
        _const_spec((1, D)), _const_spec((1, D)),
        _const_spec((N_GROUPS_B, CHUNK, CHUNK)), _const_spec((CHUNK, D)),
    ]
    slab = pltpu.VMEM((N_GROUPS_B, rows_per_chunk, CHUNK), F32)
    return pl.pallas_call(
        functools.partial(_mix_b_prompt_body, tm=tm, nb=nb),
        grid=(n_chunks, n_sub), in_specs=in_specs,
        out_specs=pl.BlockSpec((rows_per_chunk, D), lambda c, s: (c, 0)),
        out_shape=jax.ShapeDtypeStruct((T, D), F32),
        scratch_shapes=[slab, slab, slab, pltpu.VMEM((3, D, D), BF16)],
        compiler_params=pltpu.CompilerParams(dimension_semantics=("arbitrary", "arbitrary"),
                                             vmem_limit_bytes=VMEM_LIMIT),
        name="mix_b_prompt",
    )(h, lw["mix_pre_g"], lw["w_in"], lw["w_in"], lw["w_in"], lw["vn_g"], lw["vn_b"], lw["w_s"], lw["bias_full"])


def _mix_b_sample_body(ws_ref, bs_ref, h_ref, gpre_ref, wu_ref, wvg_ref, wgb_ref, vng_ref, vnb_ref,
                       ob_ref, vn_ref, *, nb, L):
    su, vn = _mix_b_tokens(h_ref[...], gpre_ref[...], wu_ref[...].astype(BF16), wvg_ref[...].astype(BF16),
                           wgb_ref[...].astype(BF16), vng_ref[...], vnb_ref[...])
    vn_ref[...] = vn
    for t in range(L):
        cols = []
        for g in range(N_GROUPS_B):
            acc = jnp.full((nb, CHUNK), bs_ref[g * L + t], F32)
            for s in range(t + 1):
                acc = acc + ws_ref[(g * L + t) * L + s] * vn[s * nb:(s + 1) * nb, g * CHUNK:(g + 1) * CHUNK]
            cols.append(acc)
        ob_ref[t * nb:(t + 1) * nb, :] = jnp.concatenate(cols, axis=1) * su[t * nb:(t + 1) * nb]


def _mix_b_sample(h, lw, *, nb, L):
    T, D = h.shape
    wcol = lambda j: _layer_spec(lw["layer"], D, D, j)
    smem = pl.BlockSpec(memory_space=pltpu.SMEM)
    in_specs = [smem, smem, _const_spec((T, D)), _const_spec((1, D)), wcol(3), wcol(4), wcol(6),
                _const_spec((1, D)), _const_spec((1, D))]
    full = pl.BlockSpec((T, D), lambda i: (0, 0))
    return pl.pallas_call(
        functools.partial(_mix_b_sample_body, nb=nb, L=L),
        grid=(1,), in_specs=in_specs, out_specs=[full, full],
        out_shape=[jax.ShapeDtypeStruct((T, D), F32)] * 2,
        compiler_params=pltpu.CompilerParams(dimension_semantics=("arbitrary",), vmem_limit_bytes=VMEM_LIMIT),
        name="mix_b_sample",
    )(lw["ws_head"], lw["bs_head"], h, lw["mix_pre_g"], lw["w_in"], lw["w_in"], lw["w_in"], lw["vn_g"], lw["vn_b"])


def _wkv_body(r_ref, k_ref, v_ref, w_ref, a_ref, s0_ref, kk_ref, ka_ref, rk_ref, lg_ref, lb_ref,
              o_ref, st_ref, vec_ref, y_ref, pend_ref, *, TT):
    NA, BB, K2, R = range(N_STAGED)
    HALF = HEAD // 2

    @pl.when(pl.program_id(1) == 0)
    def _():
        st_ref[...] = s0_ref[...]

    def stage(tt, p_prev):
        k = k_ref[tt]
        a = a_ref[tt]
        kk = k * kk_ref[...]
        nrm = jnp.sqrt(jnp.sum(kk * kk, axis=0, keepdims=True))
        kk = kk / jnp.maximum(nrm, 1e-12)
        p = p_prev * w_ref[tt]
        inv_p = 1.0 / p
        vec_ref[tt, NA] = -kk * p_prev
        vec_ref[tt, BB] = kk * a * inv_p
        vec_ref[tt, K2] = k * (1.0 + (a - 1.0) * ka_ref[...]) * inv_p
        vec_ref[tt, R] = r_ref[tt] * p
        return p

    def recur(cur, nxt, sa):
        sa_next = []
        for half in range(2):
            rows = slice(half * HALF, (half + 1) * HALF)
            sa_h = sa[rows]
            v_h = v_ref[cur, rows, :]
            y_h = jnp.zeros((HALF, LANES), F32)
            san_h = jnp.zeros((HALF, LANES), F32)
            for j in range(HEAD):
                jr = slice(j, j + 1)
                sn = st_ref[j, rows, :] + sa_h * vec_ref[cur, BB, jr, :] + v_h * vec_ref[cur, K2, jr, :]
                st_ref[j, rows, :] = sn
                y_h = y_h + sn * vec_ref[cur, R, jr, :]
                san_h = san_h + sn * vec_ref[nxt, NA, jr, :]
            y_ref[cur, rows, :] = y_h
            sa_next.append(san_h)
        return jnp.concatenate(sa_next, axis=0)

    def finish(tt, slot):
        y = y_ref[slot]
        mean = jnp.mean(y, axis=0, keepdims=True)
        cen = y - mean
        var = jnp.mean(cen * cen, axis=0, keepdims=True)
        yn = cen * lax.rsqrt(var + GN_EPS) * lg_ref[...] + lb_ref[...]
        bonus = jnp.sum(vec_ref[slot, R] * vec_ref[slot, K2] * rk_ref[...], axis=0, keepdims=True)
        o_ref[tt] = yn + bonus * v_ref[slot]

    p_end = lax.fori_loop(0, TT, stage, jnp.ones((HEAD, LANES), F32), unroll=min(AUX_UNROLL, TT))
    pend_ref[...] = p_end

    sa0 = jnp.zeros((HEAD, LANES), F32)
    for j in range(HEAD):
        sa0 = sa0 + st_ref[j] * vec_ref[0, NA, j:j + 1, :]

    lax.fori_loop(0, TT, lambda tt, sa: recur(tt, jnp.minimum(tt + 1, TT - 1), sa), sa0)

    for j in range(HEAD):
        st_ref[j] = st_ref[j] * pend_ref[j:j + 1, :]

    def finish_step(tt, carry):
        finish(tt, tt)
        return carry

    lax.fori_loop(0, TT, finish_step, 0, unroll=min(AUX_UNROLL, TT))


def _wkv(r, k, v, w, a, s0, g0, lw, *, L, nb, TT):
    G = nb // SUBLANES
    tile = pl.BlockSpec((TT, None, HEAD, LANES), lambda g, t: (t, g, 0, 0))
    st_in = pl.BlockSpec((None, HEAD, HEAD, LANES), lambda g, t: (g0 + g, 0, 0, 0))
    st = pl.BlockSpec((None, HEAD, HEAD, LANES), lambda g, t: (g, 0, 0, 0))
    par = pl.BlockSpec((HEAD, LANES), lambda g, t: (0, 0))
    return pl.pallas_call(
        functools.partial(_wkv_body, TT=TT),
        grid=(G, L // TT),
        in_specs=[tile] * 5 + [st_in] + [par] * 5,
        out_specs=[tile, st],
        out_shape=[jax.ShapeDtypeStruct((L, G, HEAD, LANES), F32),
                   jax.ShapeDtypeStruct((G, HEAD, HEAD, LANES), F32)],
        scratch_shapes=[pltpu.VMEM((TT, N_STAGED, HEAD, LANES), F32), pltpu.VMEM((TT, HEAD, LANES), F32),
                        pltpu.VMEM((HEAD, LANES), F32)],
        compiler_params=pltpu.CompilerParams(dimension_semantics=("arbitrary", "arbitrary"),
                                             vmem_limit_bytes=VMEM_LIMIT),
        name="wkv",
    )(r, k, v, w, a, s0, lw["kk_l"], lw["ka_l"], lw["rk_l"], lw["lg_l"], lw["lb_l"])


def _post_body(h_ref, o_ref, onext_ref, ga_ref, ob_ref, p_ref, gpost_ref, wout_ref, gfpre_ref, wup_ref,
               wdown_ref, gfpost_ref, wpg_ref, wpe_ref, out_ref, o_scr, *, tm, nb):
    def token_rows(ref):
        return [_from_lanes(ref[s, g]) for s, g, _ in _slabs(tm, nb)]

    @pl.when(pl.program_id(0) == 0)
    def _():
        o_scr[...] = jnp.concatenate(token_rows(o_ref), axis=0)

    mixed = ga_ref[...] * o_scr[...] + ob_ref[...]
    h = h_ref[...] + _rms(_dot(mixed.astype(BF16), wout_ref[...]), gpost_ref[...])
    z = _dot(_rms(h, gfpre_ref[...]).astype(BF16), wup_ref[...])
    z = jnp.square(jnp.maximum(z, 0.0))
    h = h + _rms(_dot(z.astype(BF16), wdown_ref[...]), gfpost_ref[...])
    nxt = token_rows(onext_ref)
    o_scr[...] = jnp.concatenate(nxt, axis=0)
    gate = jax.nn.sigmoid(_dot(h.astype(BF16), wpg_ref[...]))
    out_ref[...] = _after(h + gate * _dot(p_ref[...].astype(BF16), wpe_ref[...]), _exact_zero(nxt))


def _post(h, o, ga, ob, p, lw, *, tm, nb):
    T, D = h.shape
    tok = pl.BlockSpec((tm, D), lambda i: (i, 0))
    n_tiles = T // tm
    lane_shape = (tm // nb, nb // SUBLANES, HEAD, LANES)
    lane = pl.BlockSpec(lane_shape, lambda i: (i, 0, 0, 0))
    lane_next = pl.BlockSpec(lane_shape, lambda i: (jnp.minimum(i + 1, n_tiles - 1), 0, 0, 0))
    in_specs = [tok, lane, lane_next, tok, tok, pl.BlockSpec((tm, p.shape[1]), lambda i: (i, 0)),
                _const_spec((1, D)), _layer_spec(lw["layer"], *lw["w_out"].shape[1:]),
                _const_spec((1, D)), _layer_spec(lw["layer"], *lw["w_up"].shape[1:]),
                _layer_spec(lw["layer"], *lw["w_down"].shape[1:]),
                _const_spec((1, D)), _layer_spec(lw["layer"], *lw["w_pg"].shape[1:]),
                _layer_spec(lw["layer"], *lw["w_pe"].shape[1:])]
    return pl.pallas_call(
        functools.partial(_post_body, tm=tm, nb=nb), grid=(n_tiles,), in_specs=in_specs, out_specs=tok,
        out_shape=jax.ShapeDtypeStruct((T, D), F32),
        scratch_shapes=[pltpu.VMEM((tm, D), F32)],
        compiler_params=pltpu.CompilerParams(dimension_semantics=("arbitrary",), vmem_limit_bytes=VMEM_LIMIT),
        name="post",
    )(h, o, o, ga, ob, p, lw["mix_post_g"], lw["w_out"], lw["ffn_pre_g"], lw["w_up"], lw["w_down"],
      lw["ffn_post_g"], lw["w_pg"], lw["w_pe"])


def _lane_param(x, n_heads):
    xh = jnp.swapaxes(x.reshape(n_heads // 2, 2, HEAD), 0, 1).reshape(n_heads, HEAD)
    return jnp.repeat(xh.T, SUBLANES, axis=1).astype(F32)


def _state_to_lanes(s):
    nb, nh = s.shape[0], s.shape[1]
    s = s.reshape(nb // SUBLANES, SUBLANES, nh // 2, 2, HEAD, HEAD)
    return jnp.transpose(s, (0, 5, 4, 3, 2, 1)).reshape(nb // SUBLANES, HEAD, HEAD, nh * SUBLANES)


def _state_from_lanes(s, nh):
    G = s.shape[0]
    s = jnp.transpose(s.reshape(G, HEAD, HEAD, 2, nh // 2, SUBLANES), (0, 5, 4, 3, 2, 1))
    return s.reshape(G * SUBLANES, nh, HEAD, HEAD)


def _layer(h, p, s0, g0, init_xn, lw, *, L, nb, prompt):
    if prompt:
        tm, P = 256, BF16_ROWS
        r, k, v, w, a, ga, last = _mix_a(h, init_xn, lw, tm=tm, P=P, d=nb)
        ob = _mix_b_prompt(h, lw, nb=nb, tm=256)
        vn = None
        TT = WKV_BLOCK
    else:
        r, k, v, w, a, ga, last = _mix_a(h, init_xn, lw, tm=nb, P=nb, d=nb)
        ob, vn = _mix_b_sample(h, lw, nb=nb, L=L)
        TT = L
    o, sT = _wkv(r, k, v, w, a, s0, g0, lw, L=L, nb=nb, TT=TT)
    h = _post(h, o, ga, ob, p, lw, tm=min(256, h.shape[0]), nb=nb)
    return h, sT, last[-nb:], vn


def kernel(x_prompt, x_sample, state_wkv, state_shift, p_prompt, p_sample, mix_pre_g, mix_post_g, ffn_pre_g, ffn_post_g, w_in, mu_rkv, mu_wag, w0, w1, w2, a0, a1, a2, g1, g2, k_k, k_a, r_k, lnx_g, lnx_b, vn_g, vn_b, w_s, b_s, w_out, w_up, w_down, w_pe, w_pg):
    B, L, D = x_prompt.shape
    Bs, Ls, _ = x_sample.shape
    depth = w_in.shape[0]
    nh = D // HEAD
    row = lambda x: x.reshape(1, -1).astype(F32)
    tmajor = lambda x: jnp.swapaxes(x, 0, 1).reshape(-1, x.shape[-1])

    hp = tmajor(x_prompt)
    hs = tmajor(x_sample)
    zero_state = jnp.zeros((B // SUBLANES, HEAD, HEAD, nh * SUBLANES), F32)
    swap_k = lambda x: jnp.concatenate([x[..., :D], _swap_head_pairs(x[..., D:2 * D]), x[..., 2 * D:]], axis=-1)
    big = dict(w_in=w_in, w_k=_swap_head_pairs(w_in[..., D:2 * D]).astype(BF16),
               w_out=w_out.astype(BF16), w_up=w_up.astype(BF16),
               w_down=w_down.astype(BF16), w_pe=w_pe.astype(BF16), w_pg=w_pg.astype(BF16))
    s_all = _state_to_lanes(state_wkv.reshape((depth * Bs,) + state_wkv.shape[2:]))
    wkv_p, sh_p, wkv_s, sh_s, cv_s = [], [], [], [], []
    for i in range(depth):
        lw = dict(
            big, layer=i,
            mix_pre_g=row(mix_pre_g[i]), mix_post_g=row(mix_post_g[i]),
            ffn_pre_g=row(ffn_pre_g[i]), ffn_post_g=row(ffn_post_g[i]),
            mu_rkv=swap_k(row(mu_rkv[i])), mu_wag=mu_wag[i],
            w0=row(w0[i]), w1=w1[i].astype(BF16), w2=w2[i].astype(BF16),
            a0=_swap_head_pairs(row(a0[i])), a1=a1[i].astype(BF16), a2=_swap_head_pairs(a2[i]).astype(BF16),
            g1=g1[i].astype(BF16), g2=g2[i].astype(BF16),
            kk_l=_lane_param(k_k[i], nh), ka_l=_lane_param(k_a[i], nh), rk_l=_lane_param(r_k[i].reshape(-1), nh),
            lg_l=_lane_param(lnx_g[i], nh), lb_l=_lane_param(lnx_b[i], nh),
            vn_g=row(vn_g[i]), vn_b=row(vn_b[i]), w_s=w_s[i],
            bias_full=jnp.repeat(b_s[i].T, CHUNK, axis=1),
            ws_head=w_s[i][:, :Ls, :Ls].reshape(-1), bs_head=b_s[i][:, :Ls].reshape(-1),
        )
        hp, S_p, last_p, _ = _layer(hp, tmajor(p_prompt[i]), zero_state, 0, jnp.zeros((BF16_ROWS, D), F32), lw,
                                    L=L, nb=B, prompt=True)
        hs, S_s, last_s, vn_s = _layer(hs, tmajor(p_sample[i]), s_all, i * (Bs // SUBLANES), state_shift[i], lw,
                                       L=Ls, nb=Bs, prompt=False)
        wkv_p.append(S_p)
        sh_p.append(last_p)
        wkv_s.append(S_s)
        sh_s.append(last_s)
        cv_s.append(jnp.swapaxes(vn_s.reshape(Ls, Bs, D), 0, 1))
    yp = jnp.swapaxes(hp.reshape(L, B, D), 0, 1)
    ys = jnp.swapaxes(hs.reshape(Ls, Bs, D), 0, 1)
    new_wkv_p = _state_from_lanes(jnp.concatenate(wkv_p), nh).reshape((depth, B) + state_wkv.shape[2:])
    new_wkv_s = _state_from_lanes(jnp.concatenate(wkv_s), nh).reshape(state_wkv.shape)
    return (yp, ys, new_wkv_p, jnp.stack(sh_p), new_wkv_s, jnp.stack(sh_s), jnp.stack(cv_s))
```

```python
import functools

import numpy as np
import jax
import jax.numpy as jnp
from jax import lax
from jax.experimental import pallas as pl
from jax.experimental.pallas import tpu as pltpu

F32 = jnp.float32
BF16 = jnp.bfloat16

HEAD = 64
CHUNK = 128
N_GROUPS_B = 8
EPS = 1e-6
GN_EPS = 64e-5
SQRT_HALF = float(np.sqrt(0.5))
EXP_NEG_HALF = float(np.exp(-0.5))

LANES = 128
SUBLANES = 8
BF16_ROWS = 16
AUX_UNROLL = 4
N_STAGED = 4
WKV_BLOCK = 64
VMEM_LIMIT = 56 * 1024 * 1024


def _const_spec(shape):
    nd = len(shape)
    return pl.BlockSpec(shape, lambda *_: (0,) * nd, pipeline_mode=pl.Buffered(1))


def _layer_spec(layer, rows, cols, col_block=0):
    return pl.BlockSpec((None, rows, cols), lambda *_: (layer, 0, col_block), pipeline_mode=pl.Buffered(1))


def _rms(x, g):
    ms = jnp.mean(x * x, axis=-1, keepdims=True)
    return x * lax.rsqrt(ms + EPS) * g


def _gelu(x):
    return 0.5 * x * (1.0 + lax.erf(x * SQRT_HALF))


def _dot(a, b):
    return jnp.dot(a, b, preferred_element_type=F32)


def _lane_head_order(n_heads):
    return [2 * blk + par for par in range(2) for blk in range(n_heads // 2)]


def _to_lanes(x):
    heads = _lane_head_order(x.shape[1] // HEAD)
    return jnp.concatenate([x[:, h * HEAD:(h + 1) * HEAD] for h in heads], axis=0).T


def _swap_head_pairs(x):
    shape = x.shape
    return jnp.flip(x.reshape(shape[:-1] + (shape[-1] // LANES, 2, HEAD)), axis=-2).reshape(shape)


def _to_lanes2(x1, x2s):
    low = lax.broadcasted_iota(jnp.int32, (SUBLANES, LANES), 1) < HEAD
    first, second = [], []
    for blk in range(x1.shape[1] // LANES):
        a1 = x1[:, blk * LANES:(blk + 1) * LANES]
        a2 = x2s[:, blk * LANES:(blk + 1) * LANES]
        first.append(jnp.where(low, a1, a2))
        second.append(jnp.where(low, a2, a1))
    zt = jnp.concatenate(first + second, axis=0).T
    top, bottom = zt[:HEAD], zt[HEAD:]
    low = lax.broadcasted_iota(jnp.int32, (HEAD, LANES), 1) < HEAD
    return jnp.where(low, top, bottom), jnp.where(low, bottom, top)


def _from_lanes(y):
    n_heads = LANES // SUBLANES
    z = y.T
    blocks = [None] * n_heads
    for pos, h in enumerate(_lane_head_order(n_heads)):
        blocks[h] = z[pos * SUBLANES:(pos + 1) * SUBLANES, :]
    return jnp.concatenate(blocks, axis=1)


def _exact_zero(tiles):
    acc = jnp.zeros((SUBLANES, LANES), jnp.uint32)
    for t in tiles:
        acc = acc | lax.bitcast_convert_type(t[:SUBLANES, :LANES], jnp.uint32)
    acc = lax.shift_right_logical(lax.shift_right_logical(acc, jnp.uint32(16)), jnp.uint32(16))
    return lax.bitcast_convert_type(acc, F32)


def _after(x, zero):
    rows = BF16_ROWS if x.dtype == BF16 else SUBLANES
    z = jnp.concatenate([zero.astype(x.dtype)] * (rows // SUBLANES), axis=0)
    z = jnp.concatenate([z] * (x.shape[1] // LANES), axis=1)
    return x + jnp.concatenate([z] * (x.shape[0] // rows), axis=0)


def _slabs(tm, nb):
    tg = nb // SUBLANES
    return [(s, g, (s * tg + g) * SUBLANES) for s in range(tm // nb) for g in range(tg)]


def _mix_a_body(h_ref, hp_ref, init_ref, gpre_ref, wr_ref, wk_ref, wv_ref, wga_ref, mu_rkv_ref, mu_wag_ref,
                w0_ref, w1_ref, w2_ref, a0_ref, a1_ref, a2_ref, g1_ref, g2_ref,
                r_ref, k_ref, v_ref, w_ref, a_ref, ga_ref, last_ref, wbf_ref, *, tm, P, d):
    D = h_ref.shape[-1]
    slabs = _slabs(tm, d)
    i = pl.program_id(0)

    @pl.when(i == 0)
    def _():
        wbf_ref[0] = wr_ref[...].astype(BF16)
        wbf_ref[1] = wv_ref[...].astype(BF16)
        wbf_ref[2] = wga_ref[...].astype(BF16)
    g = gpre_ref[...]
    xn = _rms(h_ref[...], g)
    prev = jnp.where(i == 0, init_ref[...], _rms(hp_ref[...], g))
    xn_ext = jnp.concatenate([prev, xn], axis=0)
    xs = xn_ext[P - d:P - d + tm]
    dx = xs - xn
    mw = mu_wag_ref[...]
    xw = xn + dx * mw[0:1]
    xa = xn + dx * mw[1:2]
    xg = xn + dx * mw[2:3]
    xb_ext = xn_ext.astype(BF16)
    xb = xb_ext[P:]
    mu = mu_rkv_ref[...]

    def shifted_proj(s, lhs):
        pe = _dot(lhs, (wbf_ref[0], wk_ref[...], wbf_ref[1])[s])
        cur = pe[P:]
        prv = pe[P - d:P - d + tm]
        return cur + (prv - cur) * mu[:, s * D:(s + 1) * D]

    exact_zero, after = _exact_zero, _after

    def put_pair(x1, x2s, out1, out2):
        tiles = []
        for s, g, r0 in slabs:
            t1, t2 = _to_lanes2(x1[r0:r0 + SUBLANES], x2s[r0:r0 + SUBLANES])
            out1[s, g], out2[s, g] = t1, t2
            tiles += [t1, t2]
        return exact_zero(tiles)

    lw1 = jnp.tanh(_dot(xw.astype(BF16), w1_ref[...])).astype(BF16)
    la1 = _dot(xa.astype(BF16), a1_ref[...]).astype(BF16)
    lg1 = jax.nn.sigmoid(_dot(xg.astype(BF16), g1_ref[...])).astype(BF16)
    v = shifted_proj(2, xb_ext)
    r = shifted_proj(0, xb_ext)
    wl = w0_ref[...] + _dot(lw1, w2_ref[...])
    w = jnp.exp(-EXP_NEG_HALF * jax.nn.sigmoid(wl))
    a_s = jax.nn.sigmoid(a0_ref[...] + _dot(la1, a2_ref[...]))
    v_tiles = []
    for s, g, r0 in slabs:
        v_tiles.append(_to_lanes(v[r0:r0 + SUBLANES]))
        v_ref[s, g] = v_tiles[-1]
    k_s = shifted_proj(1, after(xb_ext, exact_zero(v_tiles)))
    z_wa = put_pair(w, a_s, w_ref, a_ref)
    gate = _dot(after(xb, z_wa), wbf_ref[2])
    z_rk = put_pair(r, k_s, r_ref, k_ref)
    ga_ref[...] = after(jax.nn.sigmoid(gate) * _dot(lg1, g2_ref[...]), z_rk)
    last_ref[...] = xn[tm - P:]


def _mix_a(h, init_xn, lw, *, tm, P, d):
    T, D = h.shape
    n_tiles = T // tm
    ratio = tm // P
    tok = pl.BlockSpec((tm, D), lambda i: (i, 0))
    in_specs = [
        tok,
        pl.BlockSpec((P, D), lambda i: (jnp.maximum(i * ratio - 1, 0), 0)),
        _const_spec((P, D)),
        _const_spec((1, D)),
        _layer_spec(lw["layer"], D, D, 0),
        _layer_spec(lw["layer"], D, D, 0),
        _layer_spec(lw["layer"], D, D, 2),
        _layer_spec(lw["layer"], D, D, 5),
        _const_spec((1, 3 * D)), _const_spec((3, D)),
        _const_spec((1, D)), _const_spec(lw["w1"].shape), _const_spec(lw["w2"].shape),
        _const_spec((1, D)), _const_spec(lw["a1"].shape), _const_spec(lw["a2"].shape),
        _const_spec(lw["g1"].shape), _const_spec(lw["g2"].shape),
    ]
    ts, tg = tm // d, d // SUBLANES
    lane = pl.BlockSpec((ts, tg, HEAD, LANES), lambda i: (i, 0, 0, 0))
    lane_shape = jax.ShapeDtypeStruct((T // d, tg, HEAD, LANES), F32)
    out_specs = [lane] * 5 + [tok, pl.BlockSpec((P, D), lambda i: (0, 0))]
    out_shape = [lane_shape] * 5 + [jax.ShapeDtypeStruct((T, D), F32), jax.ShapeDtypeStruct((P, D), F32)]
    return pl.pallas_call(
        functools.partial(_mix_a_body, tm=tm, P=P, d=d),
        grid=(n_tiles,), in_specs=in_specs, out_specs=out_specs, out_shape=out_shape,
        scratch_shapes=[pltpu.VMEM((3, D, D), BF16)],
        compiler_params=pltpu.CompilerParams(dimension_semantics=("arbitrary",), vmem_limit_bytes=VMEM_LIMIT),
        name="mix_a",
    )(h, h, init_xn, lw["mix_pre_g"], lw["w_in"], lw["w_k"], lw["w_in"], lw["w_in"], lw["mu_rkv"], lw["mu_wag"],
      lw["w0"], lw["w1"], lw["w2"], lw["a0"], lw["a1"], lw["a2"], lw["g1"], lw["g2"])


def _mix_b_tokens(h, gpre, wu, wvg, wgb, vng, vnb):
    xb = _rms(h, gpre).astype(BF16)
    gv = _gelu(_dot(xb, wvg))
    mean = jnp.mean(gv, axis=-1, keepdims=True)
    cen = gv - mean
    var = jnp.mean(cen * cen, axis=-1, keepdims=True)
    u = _gelu(_dot(xb, wu))
    vn = cen * lax.rsqrt(var + EPS) * vng + vnb
    su = jax.nn.sigmoid(_dot(xb, wgb)) * u
    return su, vn


def _mix_b_prompt_body(h_ref, gpre_ref, wu_ref, wvg_ref, wgb_ref, vng_ref, vnb_ref, ws_ref, bias_ref,
                       ob_ref, vn_scr, su_scr, cm_scr, wbf_ref, *, tm, nb):
    s = pl.program_id(1)
    n_sub = pl.num_programs(1)

    @pl.when((pl.program_id(0) == 0) & (s == 0))
    def _():
        wbf_ref[0] = wu_ref[...].astype(BF16)
        wbf_ref[1] = wvg_ref[...].astype(BF16)
        wbf_ref[2] = wgb_ref[...].astype(BF16)

    su, vn = _mix_b_tokens(h_ref[...], gpre_ref[...], wbf_ref[0], wbf_ref[1], wbf_ref[2],
                           vng_ref[...], vnb_ref[...])
    row0 = pl.multiple_of(s * tm, tm)
    for g in range(N_GROUPS_B):
        vn_scr[g, pl.ds(row0, tm), :] = vn[:, g * CHUNK:(g + 1) * CHUNK]
        su_scr[g, pl.ds(row0, tm), :] = su[:, g * CHUNK:(g + 1) * CHUNK]

    @pl.when(s == n_sub - 1)
    def _():
        row = lax.broadcasted_iota(jnp.int32, (CHUNK, CHUNK), 0)
        col = lax.broadcasted_iota(jnp.int32, (CHUNK, CHUNK), 1)
        for g in range(N_GROUPS_B):
            wm = jnp.where(row >= col, ws_ref[g], 0.0).astype(BF16)
            bias = bias_ref[:, g * CHUNK:(g + 1) * CHUNK]
            seq_rows = [pl.ds(b, CHUNK, stride=nb) for b in range(nb)]
            vb = jnp.concatenate([vn_scr[g, rows, :].astype(BF16) for rows in seq_rows], axis=1)
            mixed = _dot(wm, vb)
            for b, rows in enumerate(seq_rows):
                cm_scr[g, rows, :] = (mixed[:, b * CHUNK:(b + 1) * CHUNK] + bias) * su_scr[g, rows, :]
        for g in range(N_GROUPS_B):
            ob_ref[:, g * CHUNK:(g + 1) * CHUNK] = cm_scr[g]


def _mix_b_prompt(h, lw, *, nb, tm):
    T, D = h.shape
    rows_per_chunk = CHUNK * nb
    n_chunks = T // rows_per_chunk
    n_sub = rows_per_chunk // tm
    wcol = lambda j: _layer_spec(lw["layer"], D, D, j)
    in_specs = [
        pl.BlockSpec((tm, D), lambda c, s: (c * n_sub + s, 0)),
        _const_spec((1, D)), wcol(3), wcol(4), wcol(6),
        _const_spec((1, D)), _const_spec((1, D)),
        _const_spec((N_GROUPS_B, CHUNK, CHUNK)), _const_spec((CHUNK, D)),
    ]
    slab = pltpu.VMEM((N_GROUPS_B, rows_per_chunk, CHUNK), F32)
    return pl.pallas_call(
        functools.partial(_mix_b_prompt_body, tm=tm, nb=nb),
        grid=(n_chunks, n_sub), in_specs=in_specs,
        out_specs=pl.BlockSpec((rows_per_chunk, D), lambda c, s: (c, 0)),
        out_shape=jax.ShapeDtypeStruct((T, D), F32),
        scratch_shapes=[slab, slab, slab, pltpu.VMEM((3, D, D), BF16)],
        compiler_params=pltpu.CompilerParams(dimension_semantics=("arbitrary", "arbitrary"),
                                             vmem_limit_bytes=VMEM_LIMIT),
        name="mix_b_prompt",
    )(h, lw["mix_pre_g"], lw["w_in"], lw["w_in"], lw["w_in"], lw["vn_g"], lw["vn_b"], lw["w_s"], lw["bias_full"])


def _mix_b_sample_body(ws_ref, bs_ref, h_ref, gpre_ref, wu_ref, wvg_ref, wgb_ref, vng_ref, vnb_ref,
                       ob_ref, vn_ref, *, nb, L):
    su, vn = _mix_b_tokens(h_ref[...], gpre_ref[...], wu_ref[...].astype(BF16), wvg_ref[...].astype(BF16),
                           wgb_ref[...].astype(BF16), vng_ref[...], vnb_ref[...])
    vn_ref[...] = vn
    for t in range(L):
        cols = []
        for g in range(N_GROUPS_B):
            acc = jnp.full((nb, CHUNK), bs_ref[g * L + t], F32)
            for s in range(t + 1):
                acc = acc + ws_ref[(g * L + t) * L + s] * vn[s * nb:(s + 1) * nb, g * CHUNK:(g + 1) * CHUNK]
            cols.append(acc)
        ob_ref[t * nb:(t + 1) * nb, :] = jnp.concatenate(cols, axis=1) * su[t * nb:(t + 1) * nb]


def _mix_b_sample(h, lw, *, nb, L):
    T, D = h.shape
    wcol = lambda j: _layer_spec(lw["layer"], D, D, j)
    smem = pl.BlockSpec(memory_space=pltpu.SMEM)
    in_specs = [smem, smem, _const_spec((T, D)), _const_spec((1, D)), wcol(3), wcol(4), wcol(6),
                _const_spec((1, D)), _const_spec((1, D))]
    full = pl.BlockSpec((T, D), lambda i: (0, 0))
    return pl.pallas_call(
        functools.partial(_mix_b_sample_body, nb=nb, L=L),
        grid=(1,), in_specs=in_specs, out_specs=[full, full],
        out_shape=[jax.ShapeDtypeStruct((T, D), F32)] * 2,
        compiler_params=pltpu.CompilerParams(dimension_semantics=("arbitrary",), vmem_limit_bytes=VMEM_LIMIT),
        name="mix_b_sample",
    )(lw["ws_head"], lw["bs_head"], h, lw["mix_pre_g"], lw["w_in"], lw["w_in"], lw["w_in"], lw["vn_g"], lw["vn_b"])


def _wkv_body(r_ref, k_ref, v_ref, w_ref, a_ref, s0_ref, kk_ref, ka_ref, rk_ref, lg_ref, lb_ref,
              o_ref, st_ref, vec_ref, y_ref, pend_ref, *, TT):
    NA, BB, K2, R = range(N_STAGED)
    HALF = HEAD // 2

    @pl.when(pl.program_id(1) == 0)
    def _():
        st_ref[...] = s0_ref[...]

    def stage(tt, p_prev):
        k = k_ref[tt]
        a = a_ref[tt]
        kk = k * kk_ref[...]
        nrm = jnp.sqrt(jnp.sum(kk * kk, axis=0, keepdims=True))
        kk = kk / jnp.maximum(nrm, 1e-12)
        p = p_prev * w_ref[tt]
        inv_p = 1.0 / p
        vec_ref[tt, NA] = -kk * p_prev
        vec_ref[tt, BB] = kk * a * inv_p
        vec_ref[tt, K2] = k * (1.0 + (a - 1.0) * ka_ref[...]) * inv_p
        vec_ref[tt, R] = r_ref[tt] * p
        return p

    def recur(cur, nxt, sa):
        sa_next = []
        for half in range(2):
            rows = slice(half * HALF, (half + 1) * HALF)
            sa_h = sa[rows]
            v_h = v_ref[cur, rows, :]
            y_h = jnp.zeros((HALF, LANES), F32)
            san_h = jnp.zeros((HALF, LANES), F32)
            for j in range(HEAD):
                jr = slice(j, j + 1)
                sn = st_ref[j, rows, :] + sa_h * vec_ref[cur, BB, jr, :] + v_h * vec_ref[cur, K2, jr, :]
                st_ref[j, rows, :] = sn
                y_h = y_h + sn * vec_ref[cur, R, jr, :]
                san_h = san_h + sn * vec_ref[nxt, NA, jr, :]
            y_ref[cur, rows, :] = y_h
            sa_next.append(san_h)
        return jnp.concatenate(sa_next, axis=0)

    def finish(tt, slot):
        y = y_ref[slot]
        mean = jnp.mean(y, axis=0, keepdims=True)
        cen = y - mean
        var = jnp.mean(cen * cen, axis=0, keepdims=True)
        yn = cen * lax.rsqrt(var + GN_EPS) * lg_ref[...] + lb_ref[...]
        bonus = jnp.sum(vec_ref[slot, R] * vec_ref[slot, K2] * rk_ref[...], axis=0, keepdims=True)
        o_ref[tt] = yn + bonus * v_ref[slot]

    p_end = lax.fori_loop(0, TT, stage, jnp.ones((HEAD, LANES), F32), unroll=min(AUX_UNROLL, TT))
    pend_ref[...] = p_end

    sa0 = jnp.zeros((HEAD, LANES), F32)
    for j in range(HEAD):
        sa0 = sa0 + st_ref[j] * vec_ref[0, NA, j:j + 1, :]

    lax.fori_loop(0, TT, lambda tt, sa: recur(tt, jnp.minimum(tt + 1, TT - 1), sa), sa0)

    for j in range(HEAD):
        st_ref[j] = st_ref[j] * pend_ref[j:j + 1, :]

    def finish_step(tt, carry):
        finish(tt, tt)
        return carry

    lax.fori_loop(0, TT, finish_step, 0, unroll=min(AUX_UNROLL, TT))


def _wkv(r, k, v, w, a, s0, g0, lw, *, L, nb, TT):
    G = nb // SUBLANES
    tile = pl.BlockSpec((TT, None, HEAD, LANES), lambda g, t: (t, g, 0, 0))
    st_in = pl.BlockSpec((None, HEAD, HEAD, LANES), lambda g, t: (g0 + g, 0, 0, 0))
    st = pl.BlockSpec((None, HEAD, HEAD, LANES), lambda g, t: (g, 0, 0, 0))
    par = pl.BlockSpec((HEAD, LANES), lambda g, t: (0, 0))
    return pl.pallas_call(
        functools.partial(_wkv_body, TT=TT),
        grid=(G, L // TT),
        in_specs=[tile] * 5 + [st_in] + [par] * 5,
        out_specs=[tile, st],
        out_shape=[jax.ShapeDtypeStruct((L, G, HEAD, LANES), F32),
                   jax.ShapeDtypeStruct((G, HEAD, HEAD, LANES), F32)],
        scratch_shapes=[pltpu.VMEM((TT, N_STAGED, HEAD, LANES), F32), pltpu.VMEM((TT, HEAD, LANES), F32),
                        pltpu.VMEM((HEAD, LANES), F32)],
        compiler_params=pltpu.CompilerParams(dimension_semantics=("arbitrary", "arbitrary"),
                                             vmem_limit_bytes=VMEM_LIMIT),
        name="wkv",
    )(r, k, v, w, a, s0, lw["kk_l"], lw["ka_l"], lw["rk_l"], lw["lg_l"], lw["lb_l"])


def _post_body(h_ref, o_ref, onext_ref, ga_ref, ob_ref, p_ref, gpost_ref, wout_ref, gfpre_ref, wup_ref,
               wdown_ref, gfpost_ref, wpg_ref, wpe_ref, out_ref, o_scr, *rest, tm, nb):
    def token_rows(ref):
        return [_from_lanes(ref[s, g]) for s, g, _ in _slabs(tm, nb)]

    @pl.when(pl.program_id(0) == 0)
    def _():
        o_scr[...] = jnp.concatenate(token_rows(o_ref), axis=0)

    mixed = ga_ref[...] * o_scr[...] + ob_ref[...]
    h = h_ref[...] + _rms(_dot(mixed.astype(BF16), wout_ref[...]), gpost_ref[...])
    z = _dot(_rms(h, gfpre_ref[...]).astype(BF16), wup_ref[...])
    z = jnp.square(jnp.maximum(z, 0.0))
    h = h + _rms(_dot(z.astype(BF16), wdown_ref[...]), gfpost_ref[...])
    nxt = token_rows(onext_ref)
    o_scr[...] = jnp.concatenate(nxt, axis=0)
    gate = jax.nn.sigmoid(_dot(h.astype(BF16), wpg_ref[...]))
    res = _after(h + gate * _dot(p_ref[...].astype(BF16), wpe_ref[...]), _exact_zero(nxt))
    if not rest:
        out_ref[...] = res
        return
    (slab_scr,) = rest
    n_blk = res.shape[1] // LANES
    for blk in range(n_blk):
        slab_scr[blk] = res[:, blk * LANES:(blk + 1) * LANES]
    for b in range(nb):
        for blk in range(n_blk):
            out_ref[b, :, blk * LANES:(blk + 1) * LANES] = slab_scr[blk, pl.ds(b, tm // nb, stride=nb), :]


def _post(h, o, ga, ob, p, lw, *, tm, nb, batch_major_out=False):
    T, D = h.shape
    tok = pl.BlockSpec((tm, D), lambda i: (i, 0))
    n_tiles = T // tm
    lane_shape = (tm // nb, nb // SUBLANES, HEAD, LANES)
    lane = pl.BlockSpec(lane_shape, lambda i: (i, 0, 0, 0))
    lane_next = pl.BlockSpec(lane_shape, lambda i: (jnp.minimum(i + 1, n_tiles - 1), 0, 0, 0))
    in_specs = [tok, lane, lane_next, tok, tok, pl.BlockSpec((tm, p.shape[1]), lambda i: (i, 0)),
                _const_spec((1, D)), _layer_spec(lw["layer"], *lw["w_out"].shape[1:]),
                _const_spec((1, D)), _layer_spec(lw["layer"], *lw["w_up"].shape[1:]),
                _layer_spec(lw["layer"], *lw["w_down"].shape[1:]),
                _const_spec((1, D)), _layer_spec(lw["layer"], *lw["w_pg"].shape[1:]),
                _layer_spec(lw["layer"], *lw["w_pe"].shape[1:])]
    scratch = [pltpu.VMEM((tm, D), F32)]
    out_spec, out_shape = tok, jax.ShapeDtypeStruct((T, D), F32)
    if batch_major_out:
        scratch.append(pltpu.VMEM((D // LANES, tm, LANES), F32))
        out_spec = pl.BlockSpec((nb, tm // nb, D), lambda i: (0, i, 0))
        out_shape = jax.ShapeDtypeStruct((nb, T // nb, D), F32)
    return pl.pallas_call(
        functools.partial(_post_body, tm=tm, nb=nb), grid=(n_tiles,), in_specs=in_specs, out_specs=out_spec,
        out_shape=out_shape,
        scratch_shapes=scratch,
        compiler_params=pltpu.CompilerParams(dimension_semantics=("arbitrary",), vmem_limit_bytes=VMEM_LIMIT),
        name="post",
    )(h, o, o, ga, ob, p, lw["mix_post_g"], lw["w_out"], lw["ffn_pre_g"], lw["w_up"], lw["w_down"],
      lw["ffn_post_g"], lw["w_pg"], lw["w_pe"])


def _lane_param(x, n_heads):
    xh = jnp.swapaxes(x.reshape(n_heads // 2, 2, HEAD), 0, 1).reshape(n_heads, HEAD)
    return jnp.repeat(xh.T, SUBLANES, axis=1).astype(F32)


def _state_to_lanes(s):
    nb, nh = s.shape[0], s.shape[1]
    s = s.reshape(nb // SUBLANES, SUBLANES, nh // 2, 2, HEAD, HEAD)
    return jnp.transpose(s, (0, 5, 4, 3, 2, 1)).reshape(nb // SUBLANES, HEAD, HEAD, nh * SUBLANES)


def _state_from_lanes(s, nh):
    G = s.shape[0]
    s = jnp.transpose(s.reshape(G, HEAD, HEAD, 2, nh // 2, SUBLANES), (0, 5, 4, 3, 2, 1))
    return s.reshape(G * SUBLANES, nh, HEAD, HEAD)


def _layer(h, p, s0, g0, init_xn, lw, *, L, nb, prompt, batch_major_out=False):
    if prompt:
        tm, P = 256, BF16_ROWS
        r, k, v, w, a, ga, last = _mix_a(h, init_xn, lw, tm=tm, P=P, d=nb)
        ob = _mix_b_prompt(h, lw, nb=nb, tm=256)
        vn = None
        TT = WKV_BLOCK
    else:
        r, k, v, w, a, ga, last = _mix_a(h, init_xn, lw, tm=nb, P=nb, d=nb)
        ob, vn = _mix_b_sample(h, lw, nb=nb, L=L)
        TT = L
    o, sT = _wkv(r, k, v, w, a, s0, g0, lw, L=L, nb=nb, TT=TT)
    h = _post(h, o, ga, ob, p, lw, tm=min(256, h.shape[0]), nb=nb, batch_major_out=batch_major_out)
    return h, sT, last[-nb:], vn


def kernel(x_prompt, x_sample, state_wkv, state_shift, p_prompt, p_sample, mix_pre_g, mix_post_g, ffn_pre_g, ffn_post_g, w_in, mu_rkv, mu_wag, w0, w1, w2, a0, a1, a2, g1, g2, k_k, k_a, r_k, lnx_g, lnx_b, vn_g, vn_b, w_s, b_s, w_out, w_up, w_down, w_pe, w_pg):
    B, L, D = x_prompt.shape
    Bs, Ls, _ = x_sample.shape
    depth = w_in.shape[0]
    nh = D // HEAD
    row = lambda x: x.reshape(1, -1).astype(F32)
    tmajor = lambda x: jnp.swapaxes(x, 0, 1).reshape(-1, x.shape[-1])

    hp = tmajor(x_prompt)
    hs = tmajor(x_sample)
    zero_state = jnp.zeros((B // SUBLANES, HEAD, HEAD, nh * SUBLANES), F32)
    swap_k = lambda x: jnp.concatenate([x[..., :D], _swap_head_pairs(x[..., D:2 * D]), x[..., 2 * D:]], axis=-1)
    big = dict(w_in=w_in, w_k=_swap_head_pairs(w_in[..., D:2 * D]).astype(BF16),
               w_out=w_out.astype(BF16), w_up=w_up.astype(BF16),
               w_down=w_down.astype(BF16), w_pe=w_pe.astype(BF16), w_pg=w_pg.astype(BF16))
    s_all = _state_to_lanes(state_wkv.reshape((depth * Bs,) + state_wkv.shape[2:]))
    wkv_p, sh_p, wkv_s, sh_s, cv_s = [], [], [], [], []
    for i in range(depth):
        lw = dict(
            big, layer=i,
            mix_pre_g=row(mix_pre_g[i]), mix_post_g=row(mix_post_g[i]),
            ffn_pre_g=row(ffn_pre_g[i]), ffn_post_g=row(ffn_post_g[i]),
            mu_rkv=swap_k(row(mu_rkv[i])), mu_wag=mu_wag[i],
            w0=row(w0[i]), w1=w1[i].astype(BF16), w2=w2[i].astype(BF16),
            a0=_swap_head_pairs(row(a0[i])), a1=a1[i].astype(BF16), a2=_swap_head_pairs(a2[i]).astype(BF16),
            g1=g1[i].astype(BF16), g2=g2[i].astype(BF16),
            kk_l=_lane_param(k_k[i], nh), ka_l=_lane_param(k_a[i], nh), rk_l=_lane_param(r_k[i].reshape(-1), nh),
            lg_l=_lane_param(lnx_g[i], nh), lb_l=_lane_param(lnx_b[i], nh),
            vn_g=row(vn_g[i]), vn_b=row(vn_b[i]), w_s=w_s[i],
            bias_full=jnp.repeat(b_s[i].T, CHUNK, axis=1),
            ws_head=w_s[i][:, :Ls, :Ls].reshape(-1), bs_head=b_s[i][:, :Ls].reshape(-1),
        )
        hp, S_p, last_p, _ = _layer(hp, tmajor(p_prompt[i]), zero_state, 0, jnp.zeros((BF16_ROWS, D), F32), lw,
                                    L=L, nb=B, prompt=True, batch_major_out=(i == depth - 1))
        hs, S_s, last_s, vn_s = _layer(hs, tmajor(p_sample[i]), s_all, i * (Bs // SUBLANES), state_shift[i], lw,
                                       L=Ls, nb=Bs, prompt=False)
        wkv_p.append(S_p)
        sh_p.append(last_p)
        wkv_s.append(S_s)
        sh_s.append(last_s)
        cv_s.append(jnp.swapaxes(vn_s.reshape(Ls, Bs, D), 0, 1))
    yp = hp
    ys = jnp.swapaxes(hs.reshape(Ls, Bs, D), 0, 1)
    new_wkv_p = _state_from_lanes(jnp.concatenate(wkv_p), nh).reshape((depth, B) + state_wkv.shape[2:])
    new_wkv_s = _state_from_lanes(jnp.concatenate(wkv_s), nh).reshape(state_wkv.shape)
    return (yp, ys, new_wkv_p, jnp.stack(sh_p), new_wkv_s, jnp.stack(sh_s), jnp.stack(cv_s))
```
